```python
import numpy as np
import jax
import jax.numpy as jnp
from jax import lax

D_MODEL = 1024
BATCH = 4
SEQ = 8192
DEPTH = 4

N_MIXERS = 2
RMS_EPS = 1e-6
ROPE_THETA = 10000.0
NEG_INF = -1e30
N_NSA_LAYERS = (DEPTH + 1) // 2
N_MLA_LAYERS = DEPTH // 2

NSA_HEADS = 16
NSA_HEAD_DIM = D_MODEL // NSA_HEADS
NSA_KV_GROUPS = 4
NSA_HEADS_PER_GROUP = NSA_HEADS // NSA_KV_GROUPS
CMP_LEN = 32
CMP_STRIDE = 16
CMP_CHUNKS = CMP_LEN // CMP_STRIDE
CMP_HIDDEN = 128
SEL_LEN = 64
SEL_RATIO = SEL_LEN // CMP_STRIDE
SEL_TOPK = 16
SEL_FORCE = 1e4
WINDOW = 512
NSA_Q_BLOCK = 64
NSA_Q_DIM = NSA_HEADS * NSA_HEAD_DIM
NSA_KV_DIM = NSA_KV_GROUPS * NSA_HEAD_DIM
NSA_IN_DIM = NSA_Q_DIM + 6 * NSA_KV_DIM + 3 * NSA_HEADS

MLA_HEADS = 8
MLA_NOPE_DIM = 128
MLA_ROPE_DIM = 64
MLA_V_DIM = 128
MLA_Q_LORA = 256
MLA_KV_LORA = 256
MLA_IN_DIM = MLA_Q_LORA + MLA_KV_LORA + MLA_ROPE_DIM
ATTN_Q_BLOCK = 128

MOE_GROUPS = 8
MOE_EXPERTS_PER_GROUP = 8
MOE_EXPERTS = MOE_GROUPS * MOE_EXPERTS_PER_GROUP
MOE_TOP_K = 2
MOE_FF = 256
MOE_BLOCK = 128

kernel_name = 'hybrid_nsa_mla_hier_moe'


def rmsnorm(x, g):
    x32 = x.astype(jnp.float32)
    y = x32 * lax.rsqrt(jnp.mean(x32 * x32, axis=-1, keepdims=True) + RMS_EPS)
    return (y * g.astype(jnp.float32)).astype(x.dtype)


def rope_tables(seq, dim):
    inv = 1.0 / (ROPE_THETA ** (jnp.arange(0, dim, 2, dtype=jnp.float32) / dim))
    ang = jnp.arange(seq, dtype=jnp.float32)[:, None] * inv[None, :]
    return jnp.cos(ang), jnp.sin(ang)


def apply_rope(x, cos, sin):
    half = x.shape[-1] // 2
    x32 = x.astype(jnp.float32)
    x1, x2 = x32[..., :half], x32[..., half:]
    c, s = cos[:, None, :], sin[:, None, :]
    return jnp.concatenate([x1 * c - x2 * s, x2 * c + x1 * s], axis=-1).astype(x.dtype)


def masked_softmax(s, mask):
    s = jnp.where(mask, s.astype(jnp.float32), NEG_INF)
    m = jnp.max(s, axis=-1, keepdims=True)
    e = jnp.where(mask, jnp.exp(s - m), 0.0)
    return e / jnp.maximum(jnp.sum(e, axis=-1, keepdims=True), 1e-30)


def sel_overlap_weights():
    offs = np.arange(-(CMP_CHUNKS - 1), SEL_RATIO)
    lo = np.maximum(offs * CMP_STRIDE, 0)
    hi = np.minimum(offs * CMP_STRIDE + CMP_LEN - 1, SEL_LEN - 1)
    return offs, ((hi - lo + 1) / CMP_STRIDE).astype(np.float32)


def nsa_mixer(h, w_in, cmp_pos, cmp_w1, cmp_w2, q_norm, k_norm, w_out, cos, sin):
    B, S, _ = h.shape
    H, G, R, Dh, QB = NSA_HEADS, NSA_KV_GROUPS, NSA_HEADS_PER_GROUP, NSA_HEAD_DIM, NSA_Q_BLOCK
    scale = Dh ** -0.5
    splits = [int(v) for v in np.cumsum([NSA_Q_DIM] + [NSA_KV_DIM] * 6)]
    q, k_c, v_c, k_s, v_s, k_w, v_w, gate_logits = jnp.split(h @ w_in, splits, axis=-1)
    kv = lambda t: t.reshape(B, S, G, Dh)
    q = rmsnorm(q.reshape(B, S, H, Dh), q_norm)
    q_rot = apply_rope(q, cos, sin)

    n_chunks = S // CMP_STRIDE
    n_cmp = n_chunks - CMP_CHUNKS + 1

    def compress(t, pos, w1, w2):
        chunks = t.reshape(B, n_chunks, CMP_STRIDE, G, Dh)
        blocks = jnp.concatenate([chunks[:, i:i + n_cmp] for i in range(CMP_CHUNKS)], axis=2)
        blocks = blocks + pos[None, None, :, None, :]
        flat = blocks.transpose(0, 1, 3, 2, 4).reshape(B, n_cmp, G, CMP_LEN * Dh)
        return jax.nn.gelu(flat @ w1) @ w2

    k_cmp = rmsnorm(compress(kv(k_c), cmp_pos[0], cmp_w1[0], cmp_w2[0]), k_norm[0])
    v_cmp = compress(kv(v_c), cmp_pos[1], cmp_w1[1], cmp_w2[1])
    cmp_end = jnp.arange(n_cmp) * CMP_STRIDE + CMP_LEN - 1

    n_sel = S // SEL_LEN
    sel_k = min(SEL_TOPK, n_sel)
    to_blocks = lambda t: t.reshape(B, n_sel, SEL_LEN, G, Dh).transpose(0, 3, 1, 2, 4)
    k_sel_b = to_blocks(apply_rope(rmsnorm(kv(k_s), k_norm[1]), cos, sin))
    v_sel_b = to_blocks(kv(v_s))
    offs, ow = sel_overlap_weights()
    imp_idx = jnp.asarray(SEL_RATIO * np.arange(n_sel)[:, None] + offs[None, :] + (CMP_CHUNKS - 1))
    imp_w = jnp.asarray(ow)
    back_pad = SEL_RATIO * n_sel - n_cmp
    sel_j = jnp.arange(n_sel)
    b_idx = jnp.arange(B)[:, None, None, None]
    g_idx = jnp.arange(G)[None, :, None, None]

    k_win = apply_rope(rmsnorm(kv(k_w), k_norm[2]), cos, sin)
    pad = ((0, 0), (WINDOW, 0), (0, 0), (0, 0))
    k_win_p, v_win_p = jnp.pad(k_win, pad), jnp.pad(kv(v_w), pad)

    gates = jax.nn.sigmoid(gate_logits.astype(jnp.float32)).reshape(B, S, 3, H).astype(h.dtype)
    nqb = S // QB
    qsplit = lambda t: t.reshape((B, nqb, QB) + t.shape[2:]).swapaxes(0, 1)

    def block(args):
        q_b, qr_b, g_b, bi = args
        t = bi * QB + jnp.arange(QB)
        q_g = q_b.reshape(B, QB, G, R, Dh)
        qr_g = qr_b.reshape(B, QB, G, R, Dh)
        s_c = jnp.einsum('bqgrd,bngd->bgrqn', q_g, k_cmp) * scale
        p_c = masked_softmax(s_c, cmp_end[None, :] <= t[:, None])
        o_c = jnp.einsum('bgrqn,bngd->bqgrd', p_c.astype(v_cmp.dtype), v_cmp)
        imp = jnp.pad(p_c.sum(axis=2), ((0, 0), (0, 0), (0, 0), (CMP_CHUNKS - 1, back_pad)))
        imp = imp[..., imp_idx] @ imp_w
        cur = t // SEL_LEN
        valid = sel_j[None, :] * SEL_LEN <= t[:, None]
        forced = (sel_j[None, :] == 0) | (sel_j[None, :] == cur[:, None]) | (sel_j[None, :] == cur[:, None] - 1)
        score = jnp.where(forced, SEL_FORCE, jnp.where(valid, imp, -1.0))
        _, sel = lax.top_k(score, sel_k)
        k_g = k_sel_b[b_idx, g_idx, sel]
        v_g = v_sel_b[b_idx, g_idx, sel].reshape(B, G, QB, sel_k * SEL_LEN, Dh)
        k_pos = sel[..., None] * SEL_LEN + jnp.arange(SEL_LEN)
        s_s = jnp.einsum('bqgrd,bgqkld->bgrqkl', qr_g, k_g) * scale
        s_mask = (k_pos <= t[None, None, :, None, None]).reshape(B, G, 1, QB, sel_k * SEL_LEN)
        p_s = masked_softmax(s_s.reshape(B, G, R, QB, sel_k * SEL_LEN), s_mask)
        o_s = jnp.einsum('bgrqk,bgqkd->bqgrd', p_s.astype(v_g.dtype), v_g)
        k_wb = lax.dynamic_slice_in_dim(k_win_p, bi * QB, QB + WINDOW, axis=1)
        v_wb = lax.dynamic_slice_in_dim(v_win_p, bi * QB, QB + WINDOW, axis=1)
        kp = bi * QB - WINDOW + jnp.arange(QB + WINDOW)
        w_mask = (kp[None, :] <= t[:, None]) & (kp[None, :] > t[:, None] - WINDOW) & (kp[None, :] >= 0)
        s_w = jnp.einsum('bqgrd,bkgd->bgrqk', qr_g, k_wb) * scale
        p_w = masked_softmax(s_w, w_mask)
        o_w = jnp.einsum('bgrqk,bkgd->bqgrd', p_w.astype(v_wb.dtype), v_wb)
        return (g_b[:, :, 0, :, None] * o_c.reshape(B, QB, H, Dh)
                + g_b[:, :, 1, :, None] * o_s.reshape(B, QB, H, Dh)
                + g_b[:, :, 2, :, None] * o_w.reshape(B, QB, H, Dh))

    o = lax.map(block, (qsplit(q), qsplit(q_rot), qsplit(gates), jnp.arange(nqb)))
    return o.swapaxes(0, 1).reshape(B, S, NSA_Q_DIM) @ w_out


def mla_mixer(h, w_in, q_lat_norm, kv_lat_norm, w_uq, w_ukv, q_nope_norm, q_rope_norm,
              k_nope_norm, k_rope_norm, w_out, cos, sin):
    B, S, _ = h.shape
    H = MLA_HEADS
    c_q, c_kv, k_r = jnp.split(h @ w_in, [MLA_Q_LORA, MLA_Q_LORA + MLA_KV_LORA], axis=-1)
    q = (rmsnorm(c_q, q_lat_norm) @ w_uq).reshape(B, S, H, MLA_NOPE_DIM + MLA_ROPE_DIM)
    kvu = (rmsnorm(c_kv, kv_lat_norm) @ w_ukv).reshape(B, S, H, MLA_NOPE_DIM + MLA_V_DIM)
    q_nope = rmsnorm(q[..., :MLA_NOPE_DIM], q_nope_norm)
    q_rope = apply_rope(rmsnorm(q[..., MLA_NOPE_DIM:], q_rope_norm), cos, sin)
    k_nope = rmsnorm(kvu[..., :MLA_NOPE_DIM], k_nope_norm)
    v = kvu[..., MLA_NOPE_DIM:]
    k_rope = apply_rope(rmsnorm(k_r.reshape(B, S, 1, MLA_ROPE_DIM), k_rope_norm), cos, sin)[:, :, 0]
    scale = (MLA_NOPE_DIM + MLA_ROPE_DIM) ** -0.5
    outs = []
    for start in range(0, S, ATTN_Q_BLOCK):
        end = start + ATTN_Q_BLOCK
        s = (jnp.einsum('bqhd,bkhd->bhqk', q_nope[:, start:end], k_nope[:, :end])
             + jnp.einsum('bqhd,bkd->bhqk', q_rope[:, start:end], k_rope[:, :end])) * scale
        mask = jnp.arange(end)[None, :] <= jnp.arange(start, end)[:, None]
        p = masked_softmax(s, mask)
        outs.append(jnp.einsum('bhqk,bkhd->bqhd', p.astype(v.dtype), v[:, :end]))
    o = jnp.concatenate(outs, axis=1).reshape(B, S, H * MLA_V_DIM)
    return o @ w_out


def hier_moe(x2, w_group, b_group, w_expert, b_expert, w_gate, w_up, w_down):
    T, D = x2.shape
    g_prob = jax.nn.softmax((x2 @ w_group + b_group).astype(jnp.float32), axis=-1)
    grp = jnp.argmax(g_prob, axis=-1)
    g_w = jnp.take_along_axis(g_prob, grp[:, None], axis=-1)
    e_logits = (x2 @ w_expert + b_expert).astype(jnp.float32).reshape(T, MOE_GROUPS, MOE_EXPERTS_PER_GROUP)
    e_logits = jnp.take_along_axis(e_logits, grp[:, None, None], axis=1)[:, 0]
    top_p, top_i = lax.top_k(jax.nn.softmax(e_logits, axis=-1), MOE_TOP_K)
    weights = (g_w * top_p / jnp.sum(top_p, axis=-1, keepdims=True)).astype(x2.dtype)
    expert_id = grp[:, None] * MOE_EXPERTS_PER_GROUP + top_i
    n_assign = T * MOE_TOP_K
    flat_e = expert_id.reshape(n_assign)
    flat_tok = jnp.arange(n_assign, dtype=jnp.int32) // MOE_TOP_K
    order = jnp.argsort(flat_e)
    sorted_e = flat_e[order]
    counts = jnp.bincount(flat_e, length=MOE_EXPERTS)
    starts = jnp.cumsum(counts) - counts
    padded = (counts + MOE_BLOCK - 1) // MOE_BLOCK * MOE_BLOCK
    pad_ends = jnp.cumsum(padded)
    pad_starts = pad_ends - padded
    dest = pad_starts[sorted_e] + jnp.arange(n_assign) - starts[sorted_e]
    n_blocks = -(-n_assign // MOE_BLOCK) + MOE_EXPERTS
    n_slots = n_blocks * MOE_BLOCK
    slot_tok = jnp.full((n_slots,), T, jnp.int32).at[dest].set(flat_tok[order])
    slot_w = jnp.zeros((n_slots,), x2.dtype).at[dest].set(weights.reshape(n_assign)[order])
    block_e = jnp.minimum(jnp.searchsorted(pad_ends, jnp.arange(n_blocks) * MOE_BLOCK, side='right'),
                          MOE_EXPERTS - 1)
    x_pad = jnp.concatenate([x2, jnp.zeros((1, D), x2.dtype)], axis=0)
    xs = x_pad[slot_tok].reshape(n_blocks, MOE_BLOCK, D)

    def expert_block(args):
        xb, e = args
        return (jax.nn.silu(xb @ w_gate[e]) * (xb @ w_up[e])) @ w_down[e]

    ys = lax.map(expert_block, (xs, block_e)).reshape(n_slots, D) * slot_w[:, None]
    return jnp.zeros((T + 1, D), ys.dtype).at[slot_tok].add(ys)[:T]


def setup_inputs(seed: int = 0) -> dict:
    key = jax.random.key(seed)
    ks = iter(jax.random.split(key, 32))

    def nrm(shape, scale):
        return jax.random.normal(next(ks), shape, jnp.float32) * scale

    def gain(shape):
        return 1.0 + 0.1 * nrm(shape, 1.0)

    LA, LB, L = N_NSA_LAYERS, N_MLA_LAYERS, DEPTH
    return {
        'x': nrm((BATCH, SEQ, D_MODEL), 1.0),
        'norm_mix': gain((L, D_MODEL)),
        'norm_ffn': gain((L, D_MODEL)),
        'nsa_w_in': nrm((LA, D_MODEL, NSA_IN_DIM), D_MODEL ** -0.5),
        'nsa_cmp_pos': nrm((LA, 2, CMP_LEN, NSA_HEAD_DIM), 0.1),
        'nsa_cmp_w1': nrm((LA, 2, CMP_LEN * NSA_HEAD_DIM, CMP_HIDDEN), (CMP_LEN * NSA_HEAD_DIM) ** -0.5),
        'nsa_cmp_w2': nrm((LA, 2, CMP_HIDDEN, NSA_HEAD_DIM), CMP_HIDDEN ** -0.5),
        'nsa_q_norm': gain((LA, NSA_HEAD_DIM)),
        'nsa_k_norm': gain((LA, 3, NSA_HEAD_DIM)),
        'nsa_w_out': nrm((LA, NSA_Q_DIM, D_MODEL), NSA_Q_DIM ** -0.5),
        'mla_w_in': nrm((LB, D_MODEL, MLA_IN_DIM), D_MODEL ** -0.5),
        'mla_q_lat_norm': gain((LB, MLA_Q_LORA)),
        'mla_kv_lat_norm': gain((LB, MLA_KV_LORA)),
        'mla_w_uq': nrm((LB, MLA_Q_LORA, MLA_HEADS * (MLA_NOPE_DIM + MLA_ROPE_DIM)), MLA_Q_LORA ** -0.5),
        'mla_w_ukv': nrm((LB, MLA_KV_LORA, MLA_HEADS * (MLA_NOPE_DIM + MLA_V_DIM)), MLA_KV_LORA ** -0.5),
        'mla_q_nope_norm': gain((LB, MLA_NOPE_DIM)),
        'mla_q_rope_norm': gain((LB, MLA_ROPE_DIM)),
        'mla_k_nope_norm': gain((LB, MLA_NOPE_DIM)),
        'mla_k_rope_norm': gain((LB, MLA_ROPE_DIM)),
        'mla_w_out': nrm((LB, MLA_HEADS * MLA_V_DIM, D_MODEL), (MLA_HEADS * MLA_V_DIM) ** -0.5),
        'moe_w_group': nrm((L, D_MODEL, MOE_GROUPS), D_MODEL ** -0.5),
        'moe_b_group': nrm((L, MOE_GROUPS), 0.01),
        'moe_w_expert': nrm((L, D_MODEL, MOE_EXPERTS), D_MODEL ** -0.5),
        'moe_b_expert': nrm((L, MOE_EXPERTS), 0.01),
        'moe_w_gate': nrm((L, MOE_EXPERTS, D_MODEL, MOE_FF), D_MODEL ** -0.5),
        'moe_w_up': nrm((L, MOE_EXPERTS, D_MODEL, MOE_FF), D_MODEL ** -0.5),
        'moe_w_down': nrm((L, MOE_EXPERTS, MOE_FF, D_MODEL), MOE_FF ** -0.5),
    }


def reference(x, norm_mix, norm_ffn, nsa_w_in, nsa_cmp_pos, nsa_cmp_w1, nsa_cmp_w2, nsa_q_norm,
              nsa_k_norm, nsa_w_out, mla_w_in, mla_q_lat_norm, mla_kv_lat_norm, mla_w_uq, mla_w_ukv,
              mla_q_nope_norm, mla_q_rope_norm, mla_k_nope_norm, mla_k_rope_norm, mla_w_out,
              moe_w_group, moe_b_group, moe_w_expert, moe_b_expert, moe_w_gate, moe_w_up, moe_w_down):
    B, S, D = x.shape
    cos_a, sin_a = rope_tables(S, NSA_HEAD_DIM)
    cos_b, sin_b = rope_tables(S, MLA_ROPE_DIM)
    for i in range(DEPTH):
        j = i // N_MIXERS
        h = rmsnorm(x, norm_mix[i])
        if i % N_MIXERS == 0:
            mix = nsa_mixer(h, nsa_w_in[j], nsa_cmp_pos[j], nsa_cmp_w1[j], nsa_cmp_w2[j],
                            nsa_q_norm[j], nsa_k_norm[j], nsa_w_out[j], cos_a, sin_a)
        else:
            mix = mla_mixer(h, mla_w_in[j], mla_q_lat_norm[j], mla_kv_lat_norm[j], mla_w_uq[j],
                            mla_w_ukv[j], mla_q_nope_norm[j], mla_q_rope_norm[j], mla_k_nope_norm[j],
                            mla_k_rope_norm[j], mla_w_out[j], cos_b, sin_b)
        x = x + mix
        h = rmsnorm(x, norm_ffn[i]).reshape(B * S, D)
        x = x + hier_moe(h, moe_w_group[i], moe_b_group[i], moe_w_expert[i], moe_b_expert[i],
                         moe_w_gate[i], moe_w_up[i], moe_w_down[i]).reshape(B, S, D)
    return x
```

```python
import functools

import numpy as np
import jax
import jax.numpy as jnp
from jax import lax
from jax.experimental import pallas as pl
from jax.experimental.pallas import tpu as pltpu

RMS_EPS = 1e-6
ROPE_THETA = 10000.0
NEG_INF = -1e30

NSA_HEADS = 16
NSA_HEAD_DIM = 64
NSA_GROUPS = 4
NSA_REP = NSA_HEADS // NSA_GROUPS
CMP_LEN = 32
CMP_STRIDE = 16
CMP_HIDDEN = 128
SEL_LEN = 64
SEL_RATIO = SEL_LEN // CMP_STRIDE
SEL_TOPK = 16
SEL_FORCE = 1e4
WINDOW = 512
NSA_Q_DIM = NSA_HEADS * NSA_HEAD_DIM
NSA_KV_DIM = NSA_GROUPS * NSA_HEAD_DIM

MLA_HEADS = 8
MLA_NOPE = 128
MLA_ROPE = 64
MLA_V = 128
MLA_QK = MLA_NOPE + MLA_ROPE
MLA_Q_LORA = 256
MLA_KV_LORA = 256

MOE_GROUPS = 8
MOE_EPG = 8
MOE_EXPERTS = MOE_GROUPS * MOE_EPG
MOE_TOP_K = 2
MOE_FF = 256

MXU_DTYPE = jnp.bfloat16
ROW_TILE = 512
NSA_TQ = 256
NSA_TK = 256
MLA_TQ = 512
MLA_TK = 256
MOE_BLOCK = 128
MOE_TC = 128
VMEM_LIMIT = 56 * 1024 * 1024

_NT = (((1,), (1,)), ((), ()))
F32 = jnp.float32


def _params(*sem):
    return pltpu.CompilerParams(dimension_semantics=sem, vmem_limit_bytes=VMEM_LIMIT)


def _dot(a, b):
    return jnp.dot(a, b, preferred_element_type=F32)


def _dot_nt(a, b):
    return lax.dot_general(a, b, _NT, preferred_element_type=F32)


def _rms_lanes(x, g):
    ms = jnp.mean(x * x, axis=-1, keepdims=True)
    return x * lax.rsqrt(ms + RMS_EPS) * g


def _rms_rows(y, g):
    ms = jnp.mean(y * y, axis=-2, keepdims=True)
    return y * lax.rsqrt(ms + RMS_EPS) * g


def _rope_rows(y, cos, sin):
    half = y.shape[-2] // 2
    y1 = y[..., :half, :]
    y2 = y[..., half:, :]
    return jnp.concatenate([y1 * cos - y2 * sin, y2 * cos + y1 * sin], axis=-2)


def _nsa_proj_kernel(x_ref, g_ref, wq_ref, wk_ref, wv_ref, wg_ref, wc_ref, cos_ref, sin_ref,
                     qg_ref, kg_ref,
                     qn_ref, qr_ref, ks_ref, kw_ref, vs_ref, vw_ref, gt_ref, kc_ref, vc_ref,
                     *, tq, tk):
    tm = x_ref.shape[0]
    h = _rms_lanes(x_ref[...], g_ref[...]).astype(MXU_DTYPE)
    cos = cos_ref[...]
    sin = sin_ref[...]
    scale = NSA_HEAD_DIM ** -0.5

    q3 = _dot_nt(wq_ref[...], h).reshape(NSA_HEADS, NSA_HEAD_DIM, tm)
    qn = _rms_rows(q3, qg_ref[...])
    qr = _rope_rows(qn, cos, sin)
    qn = (qn * scale).astype(MXU_DTYPE).reshape(NSA_Q_DIM, tm)
    qr = (qr * scale).astype(MXU_DTYPE).reshape(NSA_Q_DIM, tm)
    for s in range(tm // tq):
        qn_ref[s] = qn[:, s * tq:(s + 1) * tq]
        qr_ref[s] = qr[:, s * tq:(s + 1) * tq]

    k4 = _dot_nt(wk_ref[...], h).reshape(2, NSA_GROUPS, NSA_HEAD_DIM, tm)
    for br, o_ref in ((0, ks_ref), (1, kw_ref)):
        kb = _rope_rows(_rms_rows(k4[br], kg_ref[br]), cos, sin)
        kb = kb.reshape(NSA_KV_DIM, tm).T.astype(MXU_DTYPE)
        for g in range(NSA_GROUPS):
            o_ref[g] = kb[:, g * NSA_HEAD_DIM:(g + 1) * NSA_HEAD_DIM]

    vT = _dot_nt(wv_ref[...], h).astype(MXU_DTYPE)
    for s in range(tm // tk):
        vs_ref[s] = vT[:NSA_KV_DIM, s * tk:(s + 1) * tk]
        vw_ref[s] = vT[NSA_KV_DIM:, s * tk:(s + 1) * tk]

    gT = jax.nn.sigmoid(_dot_nt(wg_ref[...], h))
    for s in range(tm // tq):
        gt_ref[s] = gT[:, s * tq:(s + 1) * tq]

    c = _dot(h, wc_ref[...])
    kc_ref[...] = c[:, :NSA_KV_DIM]
    vc_ref[...] = c[:, NSA_KV_DIM:]


def _nsa_proj(x2, g, w_in, q_norm, k_norm, cosT, sinT, B, S):
    T, D = x2.shape
    tm, tq, tk = ROW_TILE, NSA_TQ, NSA_TK
    nps = S // tm
    o = NSA_Q_DIM
    kvd = NSA_KV_DIM
    wq = w_in[:, :o].T.astype(MXU_DTYPE)
    wc = w_in[:, o:o + 2 * kvd].astype(MXU_DTYPE)
    wk = jnp.concatenate([w_in[:, o + 2 * kvd:o + 3 * kvd], w_in[:, o + 4 * kvd:o + 5 * kvd]], axis=1).T.astype(MXU_DTYPE)
    wv = jnp.concatenate([w_in[:, o + 3 * kvd:o + 4 * kvd], w_in[:, o + 5 * kvd:o + 6 * kvd]], axis=1).T.astype(MXU_DTYPE)
    wg = w_in[:, o + 6 * kvd:].T.astype(MXU_DTYPE)
    ng = wg.shape[0]
    full = lambda shape: pl.BlockSpec(shape, lambda i: (0,) * len(shape))
    out_shapes = (
        jax.ShapeDtypeStruct((T // tq, o, tq), MXU_DTYPE),
        jax.ShapeDtypeStruct((T // tq, o, tq), MXU_DTYPE),
        jax.ShapeDtypeStruct((B, NSA_GROUPS, S, NSA_HEAD_DIM), MXU_DTYPE),
        jax.ShapeDtypeStruct((B, NSA_GROUPS, S, NSA_HEAD_DIM), MXU_DTYPE),
        jax.ShapeDtypeStruct((T // tk, kvd, tk), MXU_DTYPE),
        jax.ShapeDtypeStruct((T // tk, kvd, tk), MXU_DTYPE),
        jax.ShapeDtypeStruct((T // tq, ng, tq), F32),
        jax.ShapeDtypeStruct((T, kvd), F32),
        jax.ShapeDtypeStruct((T, kvd), F32),
    )
    kspec = pl.BlockSpec((None, NSA_GROUPS, tm, NSA_HEAD_DIM), lambda i: (i // nps, 0, i % nps, 0))
    out_specs = (
        pl.BlockSpec((tm // tq, o, tq), lambda i: (i, 0, 0)),
        pl.BlockSpec((tm // tq, o, tq), lambda i: (i, 0, 0)),
        kspec, kspec,
        pl.BlockSpec((tm // tk, kvd, tk), lambda i: (i, 0, 0)),
        pl.BlockSpec((tm // tk, kvd, tk), lambda i: (i, 0, 0)),
        pl.BlockSpec((tm // tq, ng, tq), lambda i: (i, 0, 0)),
        pl.BlockSpec((tm, kvd), lambda i: (i, 0)),
        pl.BlockSpec((tm, kvd), lambda i: (i, 0)),
    )
    in_specs = [
        pl.BlockSpec((tm, D), lambda i: (i, 0)),
        full((1, D)), full(wq.shape), full(wk.shape), full(wv.shape), full(wg.shape), full(wc.shape),
        pl.BlockSpec((NSA_HEAD_DIM // 2, tm), lambda i: (0, i % nps)),
        pl.BlockSpec((NSA_HEAD_DIM // 2, tm), lambda i: (0, i % nps)),
        full((NSA_HEAD_DIM, 1)), full((2, NSA_HEAD_DIM, 1)),
    ]
    return pl.pallas_call(
        functools.partial(_nsa_proj_kernel, tq=tq, tk=tk),
        grid=(T // tm,), in_specs=in_specs, out_specs=out_specs, out_shape=out_shapes,
        compiler_params=_params("parallel"), name="nsa_proj",
    )(x2, g.reshape(1, D), wq, wk, wv, wg, wc, cosT, sinT,
      q_norm.reshape(NSA_HEAD_DIM, 1), k_norm[1:3].reshape(2, NSA_HEAD_DIM, 1))


def _compress_kernel(x_ref, pa_ref, pb_ref, w1a_ref, w1b_ref, w2t_ref, gain_ref, o_ref, *, normalize, kchunk):
    n, kdim = x_ref.shape
    hdim = w1a_ref.shape[1]
    h1 = jnp.zeros((n, hdim), F32)
    h2 = jnp.zeros((n, hdim), F32)
    for c in range(kdim // kchunk):
        sl = slice(c * kchunk, (c + 1) * kchunk)
        xs = x_ref[:, sl]
        h1 = h1 + _dot((xs + pa_ref[:, sl]).astype(MXU_DTYPE), w1a_ref[sl, :])
        h2 = h2 + _dot((xs + pb_ref[:, sl]).astype(MXU_DTYPE), w1b_ref[sl, :])
    hid = h1 + pltpu.roll(h2, n - 1, axis=0)
    act = jax.nn.gelu(hid).astype(MXU_DTYPE)
    oT = _dot_nt(w2t_ref[...], act)
    if normalize:
        o3 = _rms_rows(oT.reshape(NSA_GROUPS, NSA_HEAD_DIM, n), gain_ref[...])
        o = o3.reshape(NSA_KV_DIM, n).T.astype(MXU_DTYPE)
        for g in range(NSA_GROUPS):
            o_ref[g] = o[:, g * NSA_HEAD_DIM:(g + 1) * NSA_HEAD_DIM]
    else:
        o_ref[...] = oT.astype(MXU_DTYPE)


def _compress_weights(pos, w1, w2):
    G, Dh, H = NSA_GROUPS, NSA_HEAD_DIM, CMP_HIDDEN
    eye = jnp.eye(G, dtype=w1.dtype)

    def expand(w1_half):
        w = w1_half.reshape(CMP_STRIDE, Dh, H)
        return jnp.einsum('rdj,gk->rgdkj', w, eye).reshape(CMP_STRIDE * G * Dh, G * H).astype(MXU_DTYPE)

    half = CMP_STRIDE * Dh
    w1a, w1b = expand(w1[:half]), expand(w1[half:])
    tilepos = lambda p: jnp.broadcast_to(p[:, None, :], (CMP_STRIDE, G, Dh)).reshape(1, CMP_STRIDE * G * Dh)
    pa, pb = tilepos(pos[:CMP_STRIDE]), tilepos(pos[CMP_STRIDE:])
    w2t = jnp.einsum('jd,gk->kdgj', w2, eye).reshape(G * Dh, G * H).astype(MXU_DTYPE)
    return pa, pb, w1a, w1b, w2t


def _compress(xc, pos, w1, w2, gain, B, S, normalize):
    n = S // CMP_STRIDE
    kdim = CMP_STRIDE * NSA_KV_DIM
    xr = xc.reshape(B * n, kdim)
    pa, pb, w1a, w1b, w2t = _compress_weights(pos, w1, w2)
    full = lambda shape: pl.BlockSpec(shape, lambda b: (0,) * len(shape))
    if normalize:
        out_shape = jax.ShapeDtypeStruct((B, NSA_GROUPS, n, NSA_HEAD_DIM), MXU_DTYPE)
        out_spec = pl.BlockSpec((None, NSA_GROUPS, n, NSA_HEAD_DIM), lambda b: (b, 0, 0, 0))
    else:
        out_shape = jax.ShapeDtypeStruct((B, NSA_KV_DIM, n), MXU_DTYPE)
        out_spec = pl.BlockSpec((None, NSA_KV_DIM, n), lambda b: (b, 0, 0))
    return pl.pallas_call(
        functools.partial(_compress_kernel, normalize=normalize, kchunk=1024),
        grid=(B,),
        in_specs=[pl.BlockSpec((n, kdim), lambda b: (b, 0)), full(pa.shape), full(pb.shape),
                  full(w1a.shape), full(w1b.shape), full(w2t.shape), full((NSA_HEAD_DIM, 1))],
        out_specs=out_spec, out_shape=out_shape,
        compiler_params=_params("parallel"), name="nsa_compress",
    )(xr, pa, pb, w1a, w1b, w2t, gain.reshape(NSA_HEAD_DIM, 1))


def _stack_heads(ref):
    return jnp.concatenate([ref[r * NSA_HEAD_DIM:(r + 1) * NSA_HEAD_DIM, :] for r in range(NSA_REP)], axis=1)


def _nsa_cmp_kernel(qn_ref, kc_ref, vct_ref, wimp_ref, gt_ref, oc_ref, sel_ref, *, tq, sel_k):
    qi = pl.program_id(2)
    q0 = qi * tq
    ncmp = kc_ref.shape[0]
    nsel = sel_ref.shape[0]
    lanes = NSA_REP * tq
    qT = _stack_heads(qn_ref)
    s = _dot(kc_ref[...], qT)
    t4 = q0 + lax.broadcasted_iota(jnp.int32, (1, lanes), 1) % tq
    cmp_end = lax.broadcasted_iota(jnp.int32, (ncmp, 1), 0) * CMP_STRIDE + (CMP_LEN - 1)
    mask = cmp_end <= t4
    s = jnp.where(mask, s, NEG_INF)
    m = jnp.max(s, axis=0, keepdims=True)
    e = jnp.where(mask, jnp.exp(s - m), 0.0)
    p = e / jnp.maximum(jnp.sum(e, axis=0, keepdims=True), 1e-30)
    ocT = _dot(vct_ref[...], p.astype(MXU_DTYPE))
    for r in range(NSA_REP):
        oc_ref[r * NSA_HEAD_DIM:(r + 1) * NSA_HEAD_DIM, :] = ocT[:, r * tq:(r + 1) * tq] * gt_ref[0, r:r + 1, :]

    psum = p[:, :tq]
    for r in range(1, NSA_REP):
        psum = psum + p[:, r * tq:(r + 1) * tq]
    hi = psum.astype(MXU_DTYPE)
    lo = (psum - hi.astype(F32)).astype(MXU_DTYPE)
    imp = _dot(wimp_ref[...], hi) + _dot(wimp_ref[...], lo)

    t = q0 + lax.broadcasted_iota(jnp.int32, (1, tq), 1)
    j = lax.broadcasted_iota(jnp.int32, (nsel, 1), 0)
    cur = t // SEL_LEN
    valid = j * SEL_LEN <= t
    forced = (j == 0) | (j == cur) | (j == cur - 1)
    score = jnp.where(forced, SEL_FORCE, jnp.where(valid, imp, -1.0))
    jf = j.astype(F32)
    sel = jnp.zeros((nsel, tq), jnp.bool_)
    for _ in range(sel_k):
        mx = jnp.max(score, axis=0, keepdims=True)
        first = jnp.min(jnp.where(score == mx, jf, float(nsel)), axis=0, keepdims=True)
        hit = jf == first
        sel = sel | hit
        score = jnp.where(hit, -jnp.inf, score)
    sel_ref[...] = jnp.where(sel, 0.0, NEG_INF)


def _imp_weights(S):
    nsel, ncmp_pad = S // SEL_LEN, S // CMP_STRIDE
    ncmp = ncmp_pad - (CMP_LEN // CMP_STRIDE) + 1
    w = np.zeros((nsel, ncmp_pad), np.float32)
    for jj in range(nsel):
        for off in range(-(CMP_LEN // CMP_STRIDE - 1), SEL_RATIO):
            i = SEL_RATIO * jj + off
            lo = max(off * CMP_STRIDE, 0)
            hi = min(off * CMP_STRIDE + CMP_LEN - 1, SEL_LEN - 1)
            if 0 <= i < ncmp:
                w[jj, i] = (hi - lo + 1) / CMP_STRIDE
    return jnp.asarray(w, MXU_DTYPE)


def _nsa_cmp(qn, kcmp, vcmpT, gates5, B, S):
    tq = NSA_TQ
    nq = S // tq
    ncmp = S // CMP_STRIDE
    nsel = S // SEL_LEN
    wimp = _imp_weights(S)
    T = B * S
    qdim = NSA_REP * NSA_HEAD_DIM
    return pl.pallas_call(
        functools.partial(_nsa_cmp_kernel, tq=tq, sel_k=min(SEL_TOPK, nsel)),
        grid=(B, NSA_GROUPS, nq),
        in_specs=[
            pl.BlockSpec((None, qdim, tq), lambda b, g, i: (b * nq + i, g, 0)),
            pl.BlockSpec((None, None, ncmp, NSA_HEAD_DIM), lambda b, g, i: (b, g, 0, 0)),
            pl.BlockSpec((None, NSA_HEAD_DIM, ncmp), lambda b, g, i: (b, g, 0)),
            pl.BlockSpec((nsel, ncmp), lambda b, g, i: (0, 0)),
            pl.BlockSpec((None, 3, None, NSA_REP, tq), lambda b, g, i: (b * nq + i, 0, g, 0, 0)),
        ],
        out_specs=(
            pl.BlockSpec((None, qdim, tq), lambda b, g, i: (b * nq + i, g, 0)),
            pl.BlockSpec((None, None, None, nsel, tq), lambda b, g, i: (b, g, i, 0, 0)),
        ),
        out_shape=(
            jax.ShapeDtypeStruct((T // tq, NSA_Q_DIM, tq), F32),
            jax.ShapeDtypeStruct((B, NSA_GROUPS, nq, nsel, tq), F32),
        ),
        compiler_params=_params("parallel", "parallel", "parallel"), name="nsa_cmp_select",
    )(qn, kcmp, vcmpT, wimp, gates5)


def _flash_step(carry, s, vT, valid):
    m, l, acc = carry
    if valid is not None:
        s = jnp.where(valid, s, NEG_INF)
    m_new = jnp.maximum(m, jnp.max(s, axis=0, keepdims=True))
    alpha = jnp.exp(m - m_new)
    p = jnp.exp(s - m_new)
    if valid is not None:
        p = jnp.where(valid, p, 0.0)
    l = alpha * l + jnp.sum(p, axis=0, keepdims=True)
    acc = alpha * acc + _dot(vT, p.astype(MXU_DTYPE))
    return m_new, l, acc


def _flash_init(dv, nq):
    return (jnp.full((1, nq), NEG_INF, F32), jnp.zeros((1, nq), F32), jnp.zeros((dv, nq), F32))


def _nsa_attn_kernel(qr_ref, ks_ref, vs_ref, kw_ref, vw_ref, sel_ref, oc_ref, gt_ref, o_ref, *, tq, tk):
    qi = pl.program_id(2)
    q0 = qi * tq
    lanes = NSA_REP * tq
    dh = NSA_HEAD_DIM
    bpt = tk // SEL_LEN
    qT = _stack_heads(qr_ref)
    t4 = q0 + lax.broadcasted_iota(jnp.int32, (1, lanes), 1) % tq
    krow = lax.broadcasted_iota(jnp.int32, (tk, 1), 0)

    def sel_scores(kt):
        k = ks_ref[pl.ds(pl.multiple_of(kt * tk, tk), tk), :]
        s = _dot(k, qT)
        rows = []
        for jb in range(bpt):
            b1 = sel_ref[kt * bpt + jb]
            b4 = jnp.concatenate([b1] * NSA_REP, axis=1)
            rows.append(s[jb * SEL_LEN:(jb + 1) * SEL_LEN, :] + b4)
        return jnp.concatenate(rows, axis=0)

    n_full = q0 // tk

    def full_body(kt, carry):
        return _flash_step(carry, sel_scores(kt), vs_ref[kt], None)

    carry = lax.fori_loop(0, n_full, full_body, _flash_init(dh, lanes))
    for d in range(tq // tk):
        kt = n_full + d
        causal = (kt * tk + krow) <= t4
        carry = _flash_step(carry, sel_scores(kt), vs_ref[kt], causal)
    _, l_s, acc_s = carry

    w_lo = jnp.maximum(q0 - (WINDOW - 1), 0) // tk
    w_hi = (q0 + tq - 1) // tk + 1

    def win_body(kt, carry):
        k = kw_ref[pl.ds(pl.multiple_of(kt * tk, tk), tk), :]
        s = _dot(k, qT)
        kp = kt * tk + krow
        valid = (kp <= t4) & (kp > t4 - WINDOW)
        return _flash_step(carry, s, vw_ref[kt], valid)

    _, l_w, acc_w = lax.fori_loop(w_lo, w_hi, win_body, _flash_init(dh, lanes))

    o_s = acc_s / l_s
    o_w = acc_w / l_w
    for r in range(NSA_REP):
        ls = slice(r * tq, (r + 1) * tq)
        rs = slice(r * dh, (r + 1) * dh)
        o = oc_ref[rs, :] + gt_ref[1, r:r + 1, :] * o_s[:, ls] + gt_ref[2, r:r + 1, :] * o_w[:, ls]
        o_ref[rs, :] = o.astype(o_ref.dtype)


def _nsa_attn(qr, ks, vsT, kw, vwT, selb, ocg, gates5, B, S):
    tq, tk = NSA_TQ, NSA_TK
    nq = S // tq
    nkt = S // tk
    nsel = S // SEL_LEN
    T = B * S
    qdim = NSA_REP * NSA_HEAD_DIM
    selb6 = selb.reshape(B, NSA_GROUPS, nq, nsel, 1, tq)
    kspec = pl.BlockSpec((None, None, S, NSA_HEAD_DIM), lambda b, g, i: (b, g, 0, 0))
    vspec = pl.BlockSpec((nkt, NSA_HEAD_DIM, tk), lambda b, g, i: (b, g, 0))
    qspec = pl.BlockSpec((None, qdim, tq), lambda b, g, i: (b * nq + i, g, 0))
    return pl.pallas_call(
        functools.partial(_nsa_attn_kernel, tq=tq, tk=tk),
        grid=(B, NSA_GROUPS, nq),
        in_specs=[
            qspec, kspec, vspec, kspec, vspec,
            pl.BlockSpec((None, None, None, nsel, 1, tq), lambda b, g, i: (b, g, i, 0, 0, 0)),
            qspec,
            pl.BlockSpec((None, 3, None, NSA_REP, tq), lambda b, g, i: (b * nq + i, 0, g, 0, 0)),
        ],
        out_specs=qspec,
        out_shape=jax.ShapeDtypeStruct((T // tq, NSA_Q_DIM, tq), MXU_DTYPE),
        compiler_params=_params("parallel", "parallel", "arbitrary"), name="nsa_attn",
    )(qr, ks, vsT, kw, vwT, selb6, ocg, gates5)


def _out_proj_kernel(a_ref, x_ref, w_ref, o_ref):
    nsub, _, tq = a_ref.shape
    for s in range(nsub):
        yT = _dot(w_ref[...], a_ref[s])
        o_ref[s * tq:(s + 1) * tq, :] = x_ref[s * tq:(s + 1) * tq, :] + yT.T


def _out_proj(aT, x2, w_out):
    T, D = x2.shape
    nt, feat, tq = aT.shape
    tm = ROW_TILE
    wT = w_out.T.astype(MXU_DTYPE)
    return pl.pallas_call(
        _out_proj_kernel,
        grid=(T // tm,),
        in_specs=[pl.BlockSpec((tm // tq, feat, tq), lambda i: (i, 0, 0)),
                  pl.BlockSpec((tm, D), lambda i: (i, 0)),
                  pl.BlockSpec((D, feat), lambda i: (0, 0))],
        out_specs=pl.BlockSpec((tm, D), lambda i: (i, 0)),
        out_shape=jax.ShapeDtypeStruct((T, D), F32),
        compiler_params=_params("parallel"), name="out_proj",
    )(aT, x2, wT)


def _mla_proj_kernel(x_ref, g_ref, win_ref, wuq_ref, wukv_ref, cos_ref, sin_ref,
                     gq_ref, gkv_ref, gqn_ref, gqr_ref, gkn_ref, gkr_ref,
                     q_ref, k_ref, v_ref, *, tq, tk):
    tm = x_ref.shape[0]
    h = _rms_lanes(x_ref[...], g_ref[...]).astype(MXU_DTYPE)
    cos = cos_ref[...]
    sin = sin_ref[...]
    cT = _dot_nt(win_ref[...], h)
    cq = _rms_rows(cT[:MLA_Q_LORA], gq_ref[...]).astype(MXU_DTYPE)
    ckv = _rms_rows(cT[MLA_Q_LORA:MLA_Q_LORA + MLA_KV_LORA], gkv_ref[...]).astype(MXU_DTYPE)
    kr = _rope_rows(_rms_rows(cT[MLA_Q_LORA + MLA_KV_LORA:], gkr_ref[...]), cos, sin)

    scale = MLA_QK ** -0.5
    q3 = _dot(wuq_ref[...], cq).reshape(MLA_HEADS, MLA_QK, tm)
    qn = _rms_rows(q3[:, :MLA_NOPE, :], gqn_ref[...])
    qr = _rope_rows(_rms_rows(q3[:, MLA_NOPE:, :], gqr_ref[...]), cos, sin)
    q = (jnp.concatenate([qn, qr], axis=1) * scale).astype(MXU_DTYPE).reshape(MLA_HEADS * MLA_QK, tm)
    for s in range(tm // tq):
        q_ref[s] = q[:, s * tq:(s + 1) * tq]

    kv3 = _dot(wukv_ref[...], ckv).reshape(MLA_HEADS, MLA_NOPE + MLA_V, tm)
    vT = kv3[:, MLA_NOPE:, :].astype(MXU_DTYPE).reshape(MLA_HEADS * MLA_V, tm)
    for s in range(tm // tk):
        v_ref[s] = vT[:, s * tk:(s + 1) * tk]
    kn = _rms_rows(kv3[:, :MLA_NOPE, :], gkn_ref[...])
    kr_t = kr.T.astype(MXU_DTYPE)
    for hh in range(MLA_HEADS):
        k_ref[hh] = jnp.concatenate([kn[hh].T.astype(MXU_DTYPE), kr_t], axis=1)


def _mla_proj(x2, g, w_in, q_lat_norm, kv_lat_norm, w_uq, w_ukv, qn_norm, qr_norm, kn_norm, kr_norm,
              cosT, sinT, B, S):
    T, D = x2.shape
    tm, tq, tk = ROW_TILE, MLA_TQ, MLA_TK
    nps = S // tm
    winT = w_in.T.astype(MXU_DTYPE)
    wuqT = w_uq.T.astype(MXU_DTYPE)
    wukvT = w_ukv.T.astype(MXU_DTYPE)
    full = lambda shape: pl.BlockSpec(shape, lambda i: (0,) * len(shape))
    col = lambda v: v.reshape(-1, 1)
    return pl.pallas_call(
        functools.partial(_mla_proj_kernel, tq=tq, tk=tk),
        grid=(T // tm,),
        in_specs=[pl.BlockSpec((tm, D), lambda i: (i, 0)), full((1, D)),
                  full(winT.shape), full(wuqT.shape), full(wukvT.shape),
                  pl.BlockSpec((MLA_ROPE // 2, tm), lambda i: (0, i % nps)),
                  pl.BlockSpec((MLA_ROPE // 2, tm), lambda i: (0, i % nps)),
                  full((MLA_Q_LORA, 1)), full((MLA_KV_LORA, 1)), full((MLA_NOPE, 1)),
                  full((MLA_ROPE, 1)), full((MLA_NOPE, 1)), full((MLA_ROPE, 1))],
        out_specs=(pl.BlockSpec((tm // tq, MLA_HEADS * MLA_QK, tq), lambda i: (i, 0, 0)),
                   pl.BlockSpec((None, MLA_HEADS, tm, MLA_QK), lambda i: (i // nps, 0, i % nps, 0)),
                   pl.BlockSpec((tm // tk, MLA_HEADS * MLA_V, tk), lambda i: (i, 0, 0))),
        out_shape=(jax.ShapeDtypeStruct((T // tq, MLA_HEADS * MLA_QK, tq), MXU_DTYPE),
                   jax.ShapeDtypeStruct((B, MLA_HEADS, S, MLA_QK), MXU_DTYPE),
                   jax.ShapeDtypeStruct((T // tk, MLA_HEADS * MLA_V, tk), MXU_DTYPE)),
        compiler_params=_params("parallel"), name="mla_proj",
    )(x2, g.reshape(1, D), winT, wuqT, wukvT, cosT, sinT,
      col(q_lat_norm), col(kv_lat_norm), col(qn_norm), col(qr_norm), col(kn_norm), col(kr_norm))


def _mla_attn_kernel(q_ref, k_ref, v_ref, o_ref, *, tq, tk):
    qi = pl.program_id(2)
    q0 = qi * tq
    qT = q_ref[...]
    t = q0 + lax.broadcasted_iota(jnp.int32, (1, tq), 1)
    krow = lax.broadcasted_iota(jnp.int32, (tk, 1), 0)

    def scores(kt):
        k = k_ref[pl.ds(pl.multiple_of(kt * tk, tk), tk), :]
        return _dot(k, qT)

    n_full = q0 // tk

    def full_body(kt, carry):
        return _flash_step(carry, scores(kt), v_ref[kt], None)

    carry = lax.fori_loop(0, n_full, full_body, _flash_init(MLA_V, tq))
    for d in range(tq // tk):
        kt = n_full + d
        carry = _flash_step(carry, scores(kt), v_ref[kt], (kt * tk + krow) <= t)
    _, l, acc = carry
    o_ref[...] = (acc / l).astype(o_ref.dtype)


def _mla_attn(qT, k, vT, B, S):
    tq, tk = MLA_TQ, MLA_TK
    nq = S // tq
    nkt = S // tk
    T = B * S
    return pl.pallas_call(
        functools.partial(_mla_attn_kernel, tq=tq, tk=tk),
        grid=(B, MLA_HEADS, nq),
        in_specs=[pl.BlockSpec((None, MLA_QK, tq), lambda b, h, i: (b * nq + i, h, 0)),
                  pl.BlockSpec((None, None, S, MLA_QK), lambda b, h, i: (b, h, 0, 0)),
                  pl.BlockSpec((nkt, MLA_V, tk), lambda b, h, i: (b, h, 0))],
        out_specs=pl.BlockSpec((None, MLA_V, tq), lambda b, h, i: (b * nq + i, h, 0)),
        out_shape=jax.ShapeDtypeStruct((T // tq, MLA_HEADS * MLA_V, tq), MXU_DTYPE),
        compiler_params=_params("parallel", "parallel", "arbitrary"), name="mla_attn",
    )(qT, k, vT)


def _moe_router_kernel(x_ref, g_ref, w_ref, b_ref, id_ref, wt_ref):
    tm = x_ref.shape[0]
    h = _rms_lanes(x_ref[...], g_ref[...]).astype(MXU_DTYPE)
    lg = _dot_nt(w_ref[...], h) + b_ref[...]
    gl = lg[:MOE_GROUPS]
    ge = jnp.exp(gl - jnp.max(gl, axis=0, keepdims=True))
    gp = ge / jnp.sum(ge, axis=0, keepdims=True)
    rid = lax.broadcasted_iota(jnp.int32, (MOE_GROUPS, 1), 0)
    g_w = jnp.max(gp, axis=0, keepdims=True)
    grp = jnp.min(jnp.where(gp == g_w, rid, MOE_GROUPS), axis=0, keepdims=True)
    el = jnp.zeros((MOE_EPG, tm), F32)
    for gi in range(MOE_GROUPS):
        rows = lg[MOE_GROUPS + gi * MOE_EPG:MOE_GROUPS + (gi + 1) * MOE_EPG]
        el = jnp.where(grp == gi, rows, el)
    ee = jnp.exp(el - jnp.max(el, axis=0, keepdims=True))
    ep = ee / jnp.sum(ee, axis=0, keepdims=True)
    p1 = jnp.max(ep, axis=0, keepdims=True)
    i1 = jnp.min(jnp.where(ep == p1, rid, MOE_EPG), axis=0, keepdims=True)
    ep2 = jnp.where(rid == i1, -1.0, ep)
    p2 = jnp.max(ep2, axis=0, keepdims=True)
    i2 = jnp.min(jnp.where(ep2 == p2, rid, MOE_EPG), axis=0, keepdims=True)
    den = p1 + p2
    zi = jnp.zeros((1, tm), jnp.int32)
    zf = jnp.zeros((1, tm), F32)
    id_ref[...] = jnp.concatenate([grp * MOE_EPG + i1, grp * MOE_EPG + i2] + [zi] * 6, axis=0)
    wt_ref[...] = jnp.concatenate([g_w * p1 / den, g_w * p2 / den] + [zf] * 6, axis=0)


def _moe_router(x2, g, w_group, b_group, w_expert, b_expert):
    T, D = x2.shape
    tm = ROW_TILE
    nrow = 128
    pad = nrow - MOE_GROUPS - MOE_EXPERTS
    wT = jnp.concatenate([w_group, w_expert, jnp.zeros((D, pad), w_group.dtype)], axis=1).T.astype(MXU_DTYPE)
    bias = jnp.concatenate([b_group, b_expert, jnp.zeros((pad,), b_group.dtype)]).reshape(nrow, 1)
    ids, wts = pl.pallas_call(
        _moe_router_kernel,
        grid=(T // tm,),
        in_specs=[pl.BlockSpec((tm, D), lambda i: (i, 0)), pl.BlockSpec((1, D), lambda i: (0, 0)),
                  pl.BlockSpec((nrow, D), lambda i: (0, 0)), pl.BlockSpec((nrow, 1), lambda i: (0, 0))],
        out_specs=(pl.BlockSpec((None, 8, tm), lambda i: (i, 0, 0)), pl.BlockSpec((None, 8, tm), lambda i: (i, 0, 0))),
        out_shape=(jax.ShapeDtypeStruct((T // tm, 8, tm), jnp.int32), jax.ShapeDtypeStruct((T // tm, 8, tm), F32)),
        compiler_params=_params("parallel"), name="moe_router",
    )(x2, g.reshape(1, D), wT, bias)
    ids = ids[:, :MOE_TOP_K, :].transpose(0, 2, 1).reshape(T, MOE_TOP_K)
    wts = wts[:, :MOE_TOP_K, :].transpose(0, 2, 1).reshape(T, MOE_TOP_K)
    return ids, wts


def _moe_expert_kernel(be_ref, nu_ref, tok_ref, x_hbm, g_ref, sw_ref, wg_ref, wu_ref, wd_ref, y_ref, xbuf, sem):
    i = pl.program_id(0)
    nrows = xbuf.shape[0]

    def row_copy(r):
        return pltpu.make_async_copy(x_hbm.at[pl.ds(tok_ref[0, 0, r], 1), :], xbuf.at[pl.ds(r, 1), :], sem.at[0])

    @pl.when(i < nu_ref[0])
    def _():
        def start(r, c):
            row_copy(r).start()
            return c

        def wait(r, c):
            row_copy(r).wait()
            return c

        lax.fori_loop(0, nrows, start, 0)
        lax.fori_loop(0, nrows, wait, 0)
        h = _rms_lanes(xbuf[...], g_ref[...]).astype(MXU_DTYPE)
        a = _dot(h, wg_ref[...])
        u = _dot(h, wu_ref[...])
        mid = (jax.nn.silu(a) * u).astype(MXU_DTYPE)
        y_ref[...] = _dot(mid, wd_ref[...]) * sw_ref[...]

    @pl.when(i >= nu_ref[0])
    def _():
        y_ref[...] = jnp.zeros_like(y_ref)


def _moe_experts(x2, g, block_e, n_used, slot_tok, slot_w, w_gate, w_up, w_down):
    T, D = x2.shape
    blk = MOE_BLOCK
    n_blocks = block_e.shape[0]
    grid_spec = pltpu.PrefetchScalarGridSpec(
        num_scalar_prefetch=2,
        grid=(n_blocks,),
        in_specs=[
            pl.BlockSpec((1, 1, blk), lambda i, be, nu: (i, 0, 0), memory_space=pltpu.SMEM),
            pl.BlockSpec(memory_space=pl.ANY),
            pl.BlockSpec((1, D), lambda i, be, nu: (0, 0)),
            pl.BlockSpec((blk, 1), lambda i, be, nu: (i, 0)),
            pl.BlockSpec((None, D, MOE_FF), lambda i, be, nu: (be[i], 0, 0)),
            pl.BlockSpec((None, D, MOE_FF), lambda i, be, nu: (be[i], 0, 0)),
            pl.BlockSpec((None, MOE_FF, D), lambda i, be, nu: (be[i], 0, 0)),
        ],
        out_specs=pl.BlockSpec((blk, D), lambda i, be, nu: (i, 0)),
        scratch_shapes=[pltpu.VMEM((blk, D), F32), pltpu.SemaphoreType.DMA((1,))],
    )
    return pl.pallas_call(
        _moe_expert_kernel, grid_spec=grid_spec,
        out_shape=jax.ShapeDtypeStruct((n_blocks * blk, D), F32),
        compiler_params=_params("arbitrary"), name="moe_experts",
    )(block_e, n_used, slot_tok.reshape(n_blocks, 1, blk), x2, g.reshape(1, D), slot_w.reshape(n_blocks * blk, 1),
      w_gate.astype(MXU_DTYPE), w_up.astype(MXU_DTYPE), w_down.astype(MXU_DTYPE))


def _moe_combine_kernel(d_ref, y_hbm, x_ref, o_ref, ybuf, sem):
    tc = x_ref.shape[0]

    def row_copy(r):
        return pltpu.make_async_copy(y_hbm.at[pl.ds(d_ref[0, 0, r], 1), :], ybuf.at[pl.ds(r, 1), :], sem.at[0])

    def start(r, c):
        row_copy(r).start()
        return c

    def wait(r, c):
        row_copy(r).wait()
        return c

    lax.fori_loop(0, MOE_TOP_K * tc, start, 0)
    lax.fori_loop(0, MOE_TOP_K * tc, wait, 0)
    o_ref[...] = x_ref[...] + ybuf[:tc, :] + ybuf[tc:, :]


def _moe_combine(x2, y, dest):
    T, D = x2.shape
    tc = MOE_TC
    d3 = dest.reshape(T // tc, tc, MOE_TOP_K).transpose(0, 2, 1).reshape(T // tc, 1, MOE_TOP_K * tc)
    return pl.pallas_call(
        _moe_combine_kernel,
        grid=(T // tc,),
        in_specs=[pl.BlockSpec((1, 1, MOE_TOP_K * tc), lambda i: (i, 0, 0), memory_space=pltpu.SMEM),
                  pl.BlockSpec(memory_space=pl.ANY),
                  pl.BlockSpec((tc, D), lambda i: (i, 0))],
        out_specs=pl.BlockSpec((tc, D), lambda i: (i, 0)),
        out_shape=jax.ShapeDtypeStruct((T, D), F32),
        scratch_shapes=[pltpu.VMEM((MOE_TOP_K * tc, D), F32), pltpu.SemaphoreType.DMA((1,))],
        compiler_params=_params("arbitrary"), name="moe_combine",
    )(d3, y, x2)


def _moe_dispatch(ids, wts, T):
    n_assign = T * MOE_TOP_K
    flat_e = ids.reshape(n_assign)
    flat_tok = jnp.arange(n_assign, dtype=jnp.int32) // MOE_TOP_K
    order = jnp.argsort(flat_e)
    sorted_e = flat_e[order]
    counts = jnp.bincount(flat_e, length=MOE_EXPERTS)
    starts = jnp.cumsum(counts) - counts
    padded = (counts + MOE_BLOCK - 1) // MOE_BLOCK * MOE_BLOCK
    pad_ends = jnp.cumsum(padded)
    pad_starts = pad_ends - padded
    dest_sorted = (pad_starts[sorted_e] + jnp.arange(n_assign) - starts[sorted_e]).astype(jnp.int32)
    n_blocks = -(-n_assign // MOE_BLOCK) + MOE_EXPERTS
    n_slots = n_blocks * MOE_BLOCK
    slot_tok = jnp.zeros((n_slots,), jnp.int32).at[dest_sorted].set(flat_tok[order])
    slot_w = jnp.zeros((n_slots,), F32).at[dest_sorted].set(wts.reshape(n_assign)[order])
    block_e = jnp.minimum(jnp.searchsorted(pad_ends, jnp.arange(n_blocks) * MOE_BLOCK, side='right'),
                          MOE_EXPERTS - 1).astype(jnp.int32)
    dest = jnp.zeros((n_assign,), jnp.int32).at[order].set(dest_sorted).reshape(T, MOE_TOP_K)
    n_used = (pad_ends[-1] // MOE_BLOCK).astype(jnp.int32).reshape(1)
    return block_e, n_used, slot_tok, slot_w, dest


def _hier_moe(x2, g, w_group, b_group, w_expert, b_expert, w_gate, w_up, w_down):
    T = x2.shape[0]
    ids, wts = _moe_router(x2, g, w_group, b_group, w_expert, b_expert)
    block_e, n_used, slot_tok, slot_w, dest = _moe_dispatch(ids, wts, T)
    y = _moe_experts(x2, g, block_e, n_used, slot_tok, slot_w, w_gate, w_up, w_down)
    return _moe_combine(x2, y, dest)


def _rope_tables_T(seq, dim):
    inv = 1.0 / (ROPE_THETA ** (jnp.arange(0, dim, 2, dtype=F32) / dim))
    ang = jnp.arange(seq, dtype=F32)[:, None] * inv[None, :]
    return jnp.cos(ang).T, jnp.sin(ang).T


def _nsa_layer(x2, g, w_in, cmp_pos, cmp_w1, cmp_w2, q_norm, k_norm, w_out, cosT, sinT, B, S):
    qn, qr, ks, kw, vsT, vwT, gT, kc, vc = _nsa_proj(x2, g, w_in, q_norm, k_norm, cosT, sinT, B, S)
    kcmp = _compress(kc, cmp_pos[0], cmp_w1[0], cmp_w2[0], k_norm[0], B, S, True)
    vcmpT = _compress(vc, cmp_pos[1], cmp_w1[1], cmp_w2[1], k_norm[0], B, S, False)
    T = B * S
    gates5 = gT.reshape(T // NSA_TQ, 3, NSA_GROUPS, NSA_REP, NSA_TQ)
    ocg, selb = _nsa_cmp(qn, kcmp, vcmpT, gates5, B, S)
    aT = _nsa_attn(qr, ks, vsT, kw, vwT, selb, ocg, gates5, B, S)
    return _out_proj(aT, x2, w_out)


def _mla_layer(x2, g, w_in, q_lat_norm, kv_lat_norm, w_uq, w_ukv, qn_norm, qr_norm, kn_norm, kr_norm,
               w_out, cosT, sinT, B, S):
    qT, k, vT = _mla_proj(x2, g, w_in, q_lat_norm, kv_lat_norm, w_uq, w_ukv, qn_norm, qr_norm, kn_norm, kr_norm,
                          cosT, sinT, B, S)
    aT = _mla_attn(qT, k, vT, B, S)
    return _out_proj(aT, x2, w_out)


def kernel(x, norm_mix, norm_ffn, nsa_w_in, nsa_cmp_pos, nsa_cmp_w1, nsa_cmp_w2, nsa_q_norm, nsa_k_norm, nsa_w_out, mla_w_in, mla_q_lat_norm, mla_kv_lat_norm, mla_w_uq, mla_w_ukv, mla_q_nope_norm, mla_q_rope_norm, mla_k_nope_norm, mla_k_rope_norm, mla_w_out, moe_w_group, moe_b_group, moe_w_expert, moe_b_expert, moe_w_gate, moe_w_up, moe_w_down):
    B, S, D = x.shape
    depth = norm_mix.shape[0]
    assert S % max(ROW_TILE, NSA_TQ, MLA_TQ) == 0 and S >= WINDOW
    cosT, sinT = _rope_tables_T(S, NSA_HEAD_DIM)
    x2 = x.reshape(B * S, D)
    for i in range(depth):
        j = i // 2
        if i % 2 == 0:
            x2 = _nsa_layer(x2, norm_mix[i], nsa_w_in[j], nsa_cmp_pos[j], nsa_cmp_w1[j], nsa_cmp_w2[j],
                            nsa_q_norm[j], nsa_k_norm[j], nsa_w_out[j], cosT, sinT, B, S)
        else:
            x2 = _mla_layer(x2, norm_mix[i], mla_w_in[j], mla_q_lat_norm[j], mla_kv_lat_norm[j], mla_w_uq[j],
                            mla_w_ukv[j], mla_q_nope_norm[j], mla_q_rope_norm[j], mla_k_nope_norm[j],
                            mla_k_rope_norm[j], mla_w_out[j], cosT, sinT, B, S)
        x2 = _hier_moe(x2, norm_ffn[i], moe_w_group[i], moe_b_group[i], moe_w_expert[i], moe_b_expert[i],
                       moe_w_gate[i], moe_w_up[i], moe_w_down[i])
    return x2.reshape(B, S, D)
```

```python
import functools

import numpy as np
import jax
import jax.numpy as jnp
from jax import lax
from jax.experimental import pallas as pl
from jax.experimental.pallas import tpu as pltpu

RMS_EPS = 1e-6
ROPE_THETA = 10000.0
NEG_INF = -1e30

NSA_HEADS = 16
NSA_HEAD_DIM = 64
NSA_GROUPS = 4
NSA_REP = NSA_HEADS // NSA_GROUPS
CMP_LEN = 32
CMP_STRIDE = 16
CMP_HIDDEN = 128
SEL_LEN = 64
SEL_RATIO = SEL_LEN // CMP_STRIDE
SEL_TOPK = 16
SEL_FORCE = 1e4
WINDOW = 512
NSA_Q_DIM = NSA_HEADS * NSA_HEAD_DIM
NSA_KV_DIM = NSA_GROUPS * NSA_HEAD_DIM

MLA_HEADS = 8
MLA_NOPE = 128
MLA_ROPE = 64
MLA_V = 128
MLA_QK = MLA_NOPE + MLA_ROPE
MLA_Q_LORA = 256
MLA_KV_LORA = 256

MOE_GROUPS = 8
MOE_EPG = 8
MOE_EXPERTS = MOE_GROUPS * MOE_EPG
MOE_TOP_K = 2
MOE_FF = 256

MXU_DTYPE = jnp.bfloat16
ROW_TILE = 512
NSA_TQ = 512
NSA_TK = 256
MLA_TQ = 1024
MLA_TK = 256
MOE_BLOCK = 128
MOE_TC = 128
VMEM_LIMIT = 56 * 1024 * 1024

_NT = (((1,), (1,)), ((), ()))
F32 = jnp.float32
LOG2E = 1.4426950408889634


def _params(*sem):
    return pltpu.CompilerParams(dimension_semantics=sem, vmem_limit_bytes=VMEM_LIMIT)


def _dot(a, b):
    return jnp.dot(a, b, preferred_element_type=F32)


def _dot_nt(a, b):
    return lax.dot_general(a, b, _NT, preferred_element_type=F32)


def _rms_lanes(x, g):
    ms = jnp.mean(x * x, axis=-1, keepdims=True)
    return x * lax.rsqrt(ms + RMS_EPS) * g


def _rms_rows(y, g):
    ms = jnp.mean(y * y, axis=-2, keepdims=True)
    return y * lax.rsqrt(ms + RMS_EPS) * g


def _rope_rows(y, cos, sin):
    half = y.shape[-2] // 2
    y1 = y[..., :half, :]
    y2 = y[..., half:, :]
    return jnp.concatenate([y1 * cos - y2 * sin, y2 * cos + y1 * sin], axis=-2)


def _nsa_proj_kernel(x_ref, g_ref, wq_ref, wk_ref, wv_ref, wg_ref, wc_ref, cos_ref, sin_ref,
                     qg_ref, kg_ref, blk_ref,
                     qn_ref, qr_ref, ks_ref, kw_ref, vs_ref, vw_ref, gt_ref, kc_ref, vc_ref,
                     *, tq, tk):
    tm = x_ref.shape[0]
    h = _rms_lanes(x_ref[...], g_ref[...]).astype(MXU_DTYPE)
    cos = cos_ref[...]
    sin = sin_ref[...]
    scale = NSA_HEAD_DIM ** -0.5 * LOG2E

    q3 = _dot_nt(wq_ref[...], h).reshape(NSA_HEADS, NSA_HEAD_DIM, tm)
    qn = _rms_rows(q3, qg_ref[...])
    qr = _rope_rows(qn, cos, sin)
    qn = (qn * scale).astype(MXU_DTYPE).reshape(NSA_Q_DIM, tm)
    qr = (qr * scale).astype(MXU_DTYPE).reshape(NSA_Q_DIM, tm)
    for s in range(tm // tq):
        qn_ref[s] = qn[:, s * tq:(s + 1) * tq]
        qr_ref[s] = qr[:, s * tq:(s + 1) * tq]

    k4 = _dot_nt(wk_ref[...], h).reshape(2, NSA_GROUPS, NSA_HEAD_DIM, tm)
    for br, o_ref in ((0, ks_ref), (1, kw_ref)):
        kb = _rope_rows(_rms_rows(k4[br], kg_ref[br]), cos, sin)
        kb = kb.reshape(NSA_KV_DIM, tm).T.astype(MXU_DTYPE)
        for g in range(NSA_GROUPS):
            kg = kb[:, g * NSA_HEAD_DIM:(g + 1) * NSA_HEAD_DIM]
            o_ref[g] = jnp.concatenate([kg, blk_ref[...]], axis=1) if br == 0 else kg

    vT = _dot_nt(wv_ref[...], h).astype(MXU_DTYPE)
    for s in range(tm // tk):
        vs_ref[s] = vT[:NSA_KV_DIM, s * tk:(s + 1) * tk]
        vw_ref[s] = vT[NSA_KV_DIM:, s * tk:(s + 1) * tk]

    gT = jax.nn.sigmoid(_dot_nt(wg_ref[...], h))
    for s in range(tm // tq):
        gt_ref[s] = gT[:, s * tq:(s + 1) * tq]

    c = _dot(h, wc_ref[...])
    kc_ref[...] = c[:, :NSA_KV_DIM]
    vc_ref[...] = c[:, NSA_KV_DIM:]


def _nsa_proj(x2, g, w_in, q_norm, k_norm, cosT, sinT, B, S):
    T, D = x2.shape
    tm, tq, tk = ROW_TILE, NSA_TQ, NSA_TK
    nps = S // tm
    o = NSA_Q_DIM
    kvd = NSA_KV_DIM
    wq = w_in[:, :o].T.astype(MXU_DTYPE)
    wc = w_in[:, o:o + 2 * kvd].astype(MXU_DTYPE)
    wk = jnp.concatenate([w_in[:, o + 2 * kvd:o + 3 * kvd], w_in[:, o + 4 * kvd:o + 5 * kvd]], axis=1).T.astype(MXU_DTYPE)
    wv = jnp.concatenate([w_in[:, o + 3 * kvd:o + 4 * kvd], w_in[:, o + 5 * kvd:o + 6 * kvd]], axis=1).T.astype(MXU_DTYPE)
    wg = w_in[:, o + 6 * kvd:].T.astype(MXU_DTYPE)
    ng = wg.shape[0]
    nsel = S // SEL_LEN
    assert NSA_HEAD_DIM + nsel <= 256
    blk_onehot = jnp.asarray(np.arange(S)[:, None] // SEL_LEN == np.arange(nsel)[None, :], MXU_DTYPE)
    full = lambda shape: pl.BlockSpec(shape, lambda i: (0,) * len(shape))
    out_shapes = (
        jax.ShapeDtypeStruct((T // tq, o, tq), MXU_DTYPE),
        jax.ShapeDtypeStruct((T // tq, o, tq), MXU_DTYPE),
        jax.ShapeDtypeStruct((B, NSA_GROUPS, S, NSA_HEAD_DIM + nsel), MXU_DTYPE),
        jax.ShapeDtypeStruct((B, NSA_GROUPS, S, NSA_HEAD_DIM), MXU_DTYPE),
        jax.ShapeDtypeStruct((T // tk, kvd, tk), MXU_DTYPE),
        jax.ShapeDtypeStruct((T // tk, kvd, tk), MXU_DTYPE),
        jax.ShapeDtypeStruct((T // tq, ng, tq), F32),
        jax.ShapeDtypeStruct((T, kvd), F32),
        jax.ShapeDtypeStruct((T, kvd), F32),
    )
    kspec = pl.BlockSpec((None, NSA_GROUPS, tm, NSA_HEAD_DIM), lambda i: (i // nps, 0, i % nps, 0))
    kaspec = pl.BlockSpec((None, NSA_GROUPS, tm, NSA_HEAD_DIM + nsel), lambda i: (i // nps, 0, i % nps, 0))
    out_specs = (
        pl.BlockSpec((tm // tq, o, tq), lambda i: (i, 0, 0)),
        pl.BlockSpec((tm // tq, o, tq), lambda i: (i, 0, 0)),
        kaspec, kspec,
        pl.BlockSpec((tm // tk, kvd, tk), lambda i: (i, 0, 0)),
        pl.BlockSpec((tm // tk, kvd, tk), lambda i: (i, 0, 0)),
        pl.BlockSpec((tm // tq, ng, tq), lambda i: (i, 0, 0)),
        pl.BlockSpec((tm, kvd), lambda i: (i, 0)),
        pl.BlockSpec((tm, kvd), lambda i: (i, 0)),
    )
    in_specs = [
        pl.BlockSpec((tm, D), lambda i: (i, 0)),
        full((1, D)), full(wq.shape), full(wk.shape), full(wv.shape), full(wg.shape), full(wc.shape),
        pl.BlockSpec((NSA_HEAD_DIM // 2, tm), lambda i: (0, i % nps)),
        pl.BlockSpec((NSA_HEAD_DIM // 2, tm), lambda i: (0, i % nps)),
        full((NSA_HEAD_DIM, 1)), full((2, NSA_HEAD_DIM, 1)),
        pl.BlockSpec((tm, nsel), lambda i: (i % nps, 0)),
    ]
    return pl.pallas_call(
        functools.partial(_nsa_proj_kernel, tq=tq, tk=tk),
        grid=(T // tm,), in_specs=in_specs, out_specs=out_specs, out_shape=out_shapes,
        compiler_params=_params("parallel"), name="nsa_proj",
    )(x2, g.reshape(1, D), wq, wk, wv, wg, wc, cosT, sinT,
      q_norm.reshape(NSA_HEAD_DIM, 1), k_norm[1:3].reshape(2, NSA_HEAD_DIM, 1), blk_onehot)


def _compress_kernel(x_ref, pa_ref, pb_ref, w1a_ref, w1b_ref, w2t_ref, gain_ref, o_ref, *, normalize, kchunk):
    n, kdim = x_ref.shape
    hdim = w1a_ref.shape[1]
    h1 = jnp.zeros((n, hdim), F32)
    h2 = jnp.zeros((n, hdim), F32)
    for c in range(kdim // kchunk):
        sl = slice(c * kchunk, (c + 1) * kchunk)
        xs = x_ref[:, sl]
        h1 = h1 + _dot((xs + pa_ref[:, sl]).astype(MXU_DTYPE), w1a_ref[sl, :])
        h2 = h2 + _dot((xs + pb_ref[:, sl]).astype(MXU_DTYPE), w1b_ref[sl, :])
    hid = h1 + pltpu.roll(h2, n - 1, axis=0)
    act = jax.nn.gelu(hid).astype(MXU_DTYPE)
    oT = _dot_nt(w2t_ref[...], act)
    if normalize:
        o3 = _rms_rows(oT.reshape(NSA_GROUPS, NSA_HEAD_DIM, n), gain_ref[...])
        o = o3.reshape(NSA_KV_DIM, n).T.astype(MXU_DTYPE)
        for g in range(NSA_GROUPS):
            o_ref[g] = o[:, g * NSA_HEAD_DIM:(g + 1) * NSA_HEAD_DIM]
    else:
        o_ref[...] = oT.astype(MXU_DTYPE)


def _compress_weights(pos, w1, w2):
    G, Dh, H = NSA_GROUPS, NSA_HEAD_DIM, CMP_HIDDEN
    eye = jnp.eye(G, dtype=w1.dtype)

    def expand(w1_half):
        w = w1_half.reshape(CMP_STRIDE, Dh, H)
        return jnp.einsum('rdj,gk->rgdkj', w, eye).reshape(CMP_STRIDE * G * Dh, G * H).astype(MXU_DTYPE)

    half = CMP_STRIDE * Dh
    w1a, w1b = expand(w1[:half]), expand(w1[half:])
    tilepos = lambda p: jnp.broadcast_to(p[:, None, :], (CMP_STRIDE, G, Dh)).reshape(1, CMP_STRIDE * G * Dh)
    pa, pb = tilepos(pos[:CMP_STRIDE]), tilepos(pos[CMP_STRIDE:])
    w2t = jnp.einsum('jd,gk->kdgj', w2, eye).reshape(G * Dh, G * H).astype(MXU_DTYPE)
    return pa, pb, w1a, w1b, w2t


def _compress(xc, pos, w1, w2, gain, B, S, normalize):
    n = S // CMP_STRIDE
    kdim = CMP_STRIDE * NSA_KV_DIM
    xr = xc.reshape(B * n, kdim)
    pa, pb, w1a, w1b, w2t = _compress_weights(pos, w1, w2)
    full = lambda shape: pl.BlockSpec(shape, lambda b: (0,) * len(shape))
    if normalize:
        out_shape = jax.ShapeDtypeStruct((B, NSA_GROUPS, n, NSA_HEAD_DIM), MXU_DTYPE)
        out_spec = pl.BlockSpec((None, NSA_GROUPS, n, NSA_HEAD_DIM), lambda b: (b, 0, 0, 0))
    else:
        out_shape = jax.ShapeDtypeStruct((B, NSA_KV_DIM, n), MXU_DTYPE)
        out_spec = pl.BlockSpec((None, NSA_KV_DIM, n), lambda b: (b, 0, 0))
    return pl.pallas_call(
        functools.partial(_compress_kernel, normalize=normalize, kchunk=1024),
        grid=(B,),
        in_specs=[pl.BlockSpec((n, kdim), lambda b: (b, 0)), full(pa.shape), full(pb.shape),
                  full(w1a.shape), full(w1b.shape), full(w2t.shape), full((NSA_HEAD_DIM, 1))],
        out_specs=out_spec, out_shape=out_shape,
        compiler_params=_params("parallel"), name="nsa_compress",
    )(xr, pa, pb, w1a, w1b, w2t, gain.reshape(NSA_HEAD_DIM, 1))


def _stack_heads(ref):
    return jnp.concatenate([ref[r * NSA_HEAD_DIM:(r + 1) * NSA_HEAD_DIM, :] for r in range(NSA_REP)], axis=1)


def _nsa_cmp_kernel(qn_ref, kc_ref, vct_ref, wimp_ref, gt_ref, oc_ref, sel_ref, *, tq, sel_k):
    qi = pl.program_id(2)
    q0 = qi * tq
    ncmp = kc_ref.shape[0]
    nsel = sel_ref.shape[0]
    lanes = NSA_REP * tq
    qT = _stack_heads(qn_ref)
    s = _dot(kc_ref[...], qT)
    t4 = q0 + lax.broadcasted_iota(jnp.int32, (1, lanes), 1) % tq
    cmp_end = lax.broadcasted_iota(jnp.int32, (ncmp, 1), 0) * CMP_STRIDE + (CMP_LEN - 1)
    mask = cmp_end <= t4
    s = jnp.where(mask, s, NEG_INF)
    m = jnp.max(s, axis=0, keepdims=True)
    e = jnp.where(mask, jnp.exp2(s - m), 0.0)
    p = e / jnp.maximum(jnp.sum(e, axis=0, keepdims=True), 1e-30)
    ocT = _dot(vct_ref[...], p.astype(MXU_DTYPE))
    for r in range(NSA_REP):
        oc_ref[r * NSA_HEAD_DIM:(r + 1) * NSA_HEAD_DIM, :] = ocT[:, r * tq:(r + 1) * tq] * gt_ref[0, r:r + 1, :]

    psum = p[:, :tq]
    for r in range(1, NSA_REP):
        psum = psum + p[:, r * tq:(r + 1) * tq]
    hi = psum.astype(MXU_DTYPE)
    lo = (psum - hi.astype(F32)).astype(MXU_DTYPE)
    imp = _dot(wimp_ref[...], hi) + _dot(wimp_ref[...], lo)

    t = q0 + lax.broadcasted_iota(jnp.int32, (1, tq), 1)
    j = lax.broadcasted_iota(jnp.int32, (nsel, 1), 0)
    cur = t // SEL_LEN
    valid = j * SEL_LEN <= t
    forced = (j == 0) | (j == cur) | (j == cur - 1)
    score = jnp.where(forced, SEL_FORCE, jnp.where(valid, imp, -1.0))
    jf = j.astype(F32)
    sel = jnp.zeros((nsel, tq), jnp.bool_)
    for _ in range(sel_k):
        mx = jnp.max(score, axis=0, keepdims=True)
        first = jnp.min(jnp.where(score == mx, jf, float(nsel)), axis=0, keepdims=True)
        hit = jf == first
        sel = sel | hit
        score = jnp.where(hit, -jnp.inf, score)
    sel_ref[...] = jnp.where(sel, 0.0, NEG_INF).astype(sel_ref.dtype)


def _imp_weights(S):
    nsel, ncmp_pad = S // SEL_LEN, S // CMP_STRIDE
    ncmp = ncmp_pad - (CMP_LEN // CMP_STRIDE) + 1
    w = np.zeros((nsel, ncmp_pad), np.float32)
    for jj in range(nsel):
        for off in range(-(CMP_LEN // CMP_STRIDE - 1), SEL_RATIO):
            i = SEL_RATIO * jj + off
            lo = max(off * CMP_STRIDE, 0)
            hi = min(off * CMP_STRIDE + CMP_LEN - 1, SEL_LEN - 1)
            if 0 <= i < ncmp:
                w[jj, i] = (hi - lo + 1) / CMP_STRIDE
    return jnp.asarray(w, MXU_DTYPE)


def _nsa_cmp(qn, kcmp, vcmpT, gates5, B, S):
    tq = NSA_TQ
    nq = S // tq
    ncmp = S // CMP_STRIDE
    nsel = S // SEL_LEN
    wimp = _imp_weights(S)
    T = B * S
    qdim = NSA_REP * NSA_HEAD_DIM
    return pl.pallas_call(
        functools.partial(_nsa_cmp_kernel, tq=tq, sel_k=min(SEL_TOPK, nsel)),
        grid=(B, NSA_GROUPS, nq),
        in_specs=[
            pl.BlockSpec((None, qdim, tq), lambda b, g, i: (b * nq + i, g, 0)),
            pl.BlockSpec((None, None, ncmp, NSA_HEAD_DIM), lambda b, g, i: (b, g, 0, 0)),
            pl.BlockSpec((None, NSA_HEAD_DIM, ncmp), lambda b, g, i: (b, g, 0)),
            pl.BlockSpec((nsel, ncmp), lambda b, g, i: (0, 0)),
            pl.BlockSpec((None, 3, None, NSA_REP, tq), lambda b, g, i: (b * nq + i, 0, g, 0, 0)),
        ],
        out_specs=(
            pl.BlockSpec((None, qdim, tq), lambda b, g, i: (b * nq + i, g, 0)),
            pl.BlockSpec((None, None, None, nsel, tq), lambda b, g, i: (b, g, i, 0, 0)),
        ),
        out_shape=(
            jax.ShapeDtypeStruct((T // tq, NSA_Q_DIM, tq), F32),
            jax.ShapeDtypeStruct((B, NSA_GROUPS, nq, nsel, tq), MXU_DTYPE),
        ),
        compiler_params=_params("parallel", "parallel", "parallel"), name="nsa_cmp_select",
    )(qn, kcmp, vcmpT, wimp, gates5)


def _flash_step(carry, s, vT):
    m, l, acc = carry
    m_new = jnp.maximum(m, jnp.max(s, axis=0, keepdims=True))
    alpha = jnp.exp2(m - m_new)
    p = jnp.exp2(s - m_new)
    l = alpha * l + jnp.sum(p, axis=0, keepdims=True)
    acc = alpha * acc + _dot(vT, p.astype(MXU_DTYPE))
    return m_new, l, acc


def _flash_init(dv, nq):
    return (jnp.full((1, nq), NEG_INF, F32), jnp.zeros((1, nq), F32), jnp.zeros((dv, nq), F32))


def _flash_causal(qi, unroll, tk, scores, values, diff, s_scr, carry):
    for u in range(unroll):
        s_scr[u] = scores(u)

    def body(i, carry):
        for u in range(unroll):
            m, l, acc = carry
            s = s_scr[u]
            m_new = jnp.maximum(m, jnp.max(s, axis=0, keepdims=True))
            alpha = jnp.exp2(m - m_new)
            p = jnp.exp2(s - m_new)
            l = alpha * l + jnp.sum(p, axis=0, keepdims=True)
            s_scr[u] = scores((i + 1) * unroll + u)
            acc = alpha * acc + _dot(values(i * unroll + u), p.astype(MXU_DTYPE))
            carry = (m_new, l, acc)
        return carry

    carry = lax.fori_loop(0, qi, body, carry)
    for d in range(unroll):
        s = jnp.where(diff <= -d * tk, s_scr[d], NEG_INF)
        carry = _flash_step(carry, s, values(qi * unroll + d))
    return carry


def _nsa_attn_kernel(qr_ref, ks_ref, vs_ref, kw_ref, vw_ref, sel_ref, oc_ref, gt_ref, o_ref, s_scr, *, tq, tk):
    qi = pl.program_id(2)
    lanes = NSA_REP * tq
    dh = NSA_HEAD_DIM
    unroll = tq // tk
    qT = _stack_heads(qr_ref)
    q_aug = jnp.concatenate([qT, jnp.concatenate([sel_ref[...]] * NSA_REP, axis=1)], axis=0)
    tl = lax.broadcasted_iota(jnp.int32, (1, lanes), 1) % tq
    diff = lax.broadcasted_iota(jnp.int32, (tk, 1), 0) - tl

    def sel_scores(kt):
        return _dot(ks_ref[pl.ds(pl.multiple_of(kt * tk, tk), tk), :], q_aug)

    _, l_s, acc_s = _flash_causal(qi, unroll, tk, sel_scores, lambda kt: vs_ref[kt], diff, s_scr,
                                  _flash_init(dh, lanes))

    def win_scores(d):
        kt = qi * unroll + d
        mask = diff <= -d * tk if d >= 0 else None
        lo_thr = -d * tk - WINDOW
        if d < 0:
            lo = diff > jnp.where(kt >= 0, lo_thr, 2 ** 30)
            kt = jnp.maximum(kt, 0)
        else:
            lo = diff > lo_thr if lo_thr >= -(tq - 1) else None
        if lo is not None:
            mask = lo if mask is None else mask & lo
        s = _dot(kw_ref[pl.ds(pl.multiple_of(kt * tk, tk), tk), :], qT)
        return s if mask is None else jnp.where(mask, s, NEG_INF)

    order = [0] + list(range(1, unroll)) + list(range(-(WINDOW // tk), 0))
    carry = _flash_init(dh, lanes)
    nxt = win_scores(order[0])
    for i, d in enumerate(order):
        m, l, acc = carry
        s = nxt
        m_new = jnp.maximum(m, jnp.max(s, axis=0, keepdims=True))
        alpha = jnp.exp2(m - m_new)
        p = jnp.exp2(s - m_new)
        l = alpha * l + jnp.sum(p, axis=0, keepdims=True)
        if i + 1 < len(order):
            nxt = win_scores(order[i + 1])
        acc = alpha * acc + _dot(vw_ref[jnp.maximum(qi * unroll + d, 0)], p.astype(MXU_DTYPE))
        carry = (m_new, l, acc)
    _, l_w, acc_w = carry

    o_s = acc_s / l_s
    o_w = acc_w / l_w
    for r in range(NSA_REP):
        ls = slice(r * tq, (r + 1) * tq)
        rs = slice(r * dh, (r + 1) * dh)
        o = oc_ref[rs, :] + gt_ref[1, r:r + 1, :] * o_s[:, ls] + gt_ref[2, r:r + 1, :] * o_w[:, ls]
        o_ref[rs, :] = o.astype(o_ref.dtype)


def _nsa_attn(qr, ks, vsT, kw, vwT, selb, ocg, gates5, B, S):
    tq, tk = NSA_TQ, NSA_TK
    nq = S // tq
    nkt = S // tk
    nsel = S // SEL_LEN
    T = B * S
    qdim = NSA_REP * NSA_HEAD_DIM
    assert tq % tk == 0 and WINDOW % tk == 0 and tq <= WINDOW
    kspec = pl.BlockSpec((None, None, S, NSA_HEAD_DIM), lambda b, g, i: (b, g, 0, 0))
    kaspec = pl.BlockSpec((None, None, S, NSA_HEAD_DIM + nsel), lambda b, g, i: (b, g, 0, 0))
    vspec = pl.BlockSpec((nkt, NSA_HEAD_DIM, tk), lambda b, g, i: (b, g, 0))
    qspec = pl.BlockSpec((None, qdim, tq), lambda b, g, i: (b * nq + i, g, 0))
    return pl.pallas_call(
        functools.partial(_nsa_attn_kernel, tq=tq, tk=tk),
        grid=(B, NSA_GROUPS, nq),
        in_specs=[
            qspec, kaspec, vspec, kspec, vspec,
            pl.BlockSpec((None, None, None, nsel, tq), lambda b, g, i: (b, g, i, 0, 0)),
            qspec,
            pl.BlockSpec((None, 3, None, NSA_REP, tq), lambda b, g, i: (b * nq + i, 0, g, 0, 0)),
        ],
        out_specs=qspec,
        out_shape=jax.ShapeDtypeStruct((T // tq, NSA_Q_DIM, tq), MXU_DTYPE),
        scratch_shapes=[pltpu.VMEM((tq // tk, tk, NSA_REP * tq), F32)],
        compiler_params=_params("parallel", "parallel", "arbitrary"), name="nsa_attn",
    )(qr, ks, vsT, kw, vwT, selb, ocg, gates5)


def _out_proj_kernel(a_ref, x_ref, w_ref, o_ref):
    nsub, _, tq = a_ref.shape
    for s in range(nsub):
        yT = _dot(w_ref[...], a_ref[s])
        o_ref[s * tq:(s + 1) * tq, :] = x_ref[s * tq:(s + 1) * tq, :] + yT.T


def _out_proj(aT, x2, w_out):
    T, D = x2.shape
    nt, feat, tq = aT.shape
    tm = ROW_TILE
    wT = w_out.T.astype(MXU_DTYPE)
    return pl.pallas_call(
        _out_proj_kernel,
        grid=(T // tm,),
        in_specs=[pl.BlockSpec((tm // tq, feat, tq), lambda i: (i, 0, 0)),
                  pl.BlockSpec((tm, D), lambda i: (i, 0)),
                  pl.BlockSpec((D, feat), lambda i: (0, 0))],
        out_specs=pl.BlockSpec((tm, D), lambda i: (i, 0)),
        out_shape=jax.ShapeDtypeStruct((T, D), F32),
        compiler_params=_params("parallel"), name="out_proj",
    )(aT, x2, wT)


def _mla_proj_kernel(x_ref, g_ref, win_ref, wuq_ref, wukv_ref, cos_ref, sin_ref,
                     gq_ref, gkv_ref, gqn_ref, gqr_ref, gkn_ref, gkr_ref,
                     q_ref, k_ref, v_ref, *, tq, tk):
    tm = x_ref.shape[0]
    h = _rms_lanes(x_ref[...], g_ref[...]).astype(MXU_DTYPE)
    cos = cos_ref[...]
    sin = sin_ref[...]
    cT = _dot_nt(win_ref[...], h)
    cq = _rms_rows(cT[:MLA_Q_LORA], gq_ref[...]).astype(MXU_DTYPE)
    ckv = _rms_rows(cT[MLA_Q_LORA:MLA_Q_LORA + MLA_KV_LORA], gkv_ref[...]).astype(MXU_DTYPE)
    kr = _rope_rows(_rms_rows(cT[MLA_Q_LORA + MLA_KV_LORA:], gkr_ref[...]), cos, sin)

    scale = MLA_QK ** -0.5 * LOG2E
    q3 = _dot(wuq_ref[...], cq).reshape(MLA_HEADS, MLA_QK, tm)
    qn = _rms_rows(q3[:, :MLA_NOPE, :], gqn_ref[...])
    qr = _rope_rows(_rms_rows(q3[:, MLA_NOPE:, :], gqr_ref[...]), cos, sin)
    q = (jnp.concatenate([qn, qr], axis=1) * scale).astype(MXU_DTYPE).reshape(MLA_HEADS * MLA_QK, tm)
    for s in range(tm // tq):
        q_ref[s] = q[:, s * tq:(s + 1) * tq]

    kv3 = _dot(wukv_ref[...], ckv).reshape(MLA_HEADS, MLA_NOPE + MLA_V, tm)
    vT = kv3[:, MLA_NOPE:, :].astype(MXU_DTYPE).reshape(MLA_HEADS * MLA_V, tm)
    for s in range(tm // tk):
        v_ref[s] = vT[:, s * tk:(s + 1) * tk]
    kn = _rms_rows(kv3[:, :MLA_NOPE, :], gkn_ref[...])
    kr_t = kr.T.astype(MXU_DTYPE)
    for hh in range(MLA_HEADS):
        k_ref[hh] = jnp.concatenate([kn[hh].T.astype(MXU_DTYPE), kr_t], axis=1)


def _mla_proj(x2, g, w_in, q_lat_norm, kv_lat_norm, w_uq, w_ukv, qn_norm, qr_norm, kn_norm, kr_norm,
              cosT, sinT, B, S):
    T, D = x2.shape
    tm, tq, tk = ROW_TILE, ROW_TILE, MLA_TK
    nps = S // tm
    winT = w_in.T.astype(MXU_DTYPE)
    wuqT = w_uq.T.astype(MXU_DTYPE)
    wukvT = w_ukv.T.astype(MXU_DTYPE)
    full = lambda shape: pl.BlockSpec(shape, lambda i: (0,) * len(shape))
    col = lambda v: v.reshape(-1, 1)
    return pl.pallas_call(
        functools.partial(_mla_proj_kernel, tq=tq, tk=tk),
        grid=(T // tm,),
        in_specs=[pl.BlockSpec((tm, D), lambda i: (i, 0)), full((1, D)),
                  full(winT.shape), full(wuqT.shape), full(wukvT.shape),
                  pl.BlockSpec((MLA_ROPE // 2, tm), lambda i: (0, i % nps)),
                  pl.BlockSpec((MLA_ROPE // 2, tm), lambda i: (0, i % nps)),
                  full((MLA_Q_LORA, 1)), full((MLA_KV_LORA, 1)), full((MLA_NOPE, 1)),
                  full((MLA_ROPE, 1)), full((MLA_NOPE, 1)), full((MLA_ROPE, 1))],
        out_specs=(pl.BlockSpec((tm // tq, MLA_HEADS * MLA_QK, tq), lambda i: (i, 0, 0)),
                   pl.BlockSpec((None, MLA_HEADS, tm, MLA_QK), lambda i: (i // nps, 0, i % nps, 0)),
                   pl.BlockSpec((tm // tk, MLA_HEADS * MLA_V, tk), lambda i: (i, 0, 0))),
        out_shape=(jax.ShapeDtypeStruct((T // tq, MLA_HEADS * MLA_QK, tq), MXU_DTYPE),
                   jax.ShapeDtypeStruct((B, MLA_HEADS, S, MLA_QK), MXU_DTYPE),
                   jax.ShapeDtypeStruct((T // tk, MLA_HEADS * MLA_V, tk), MXU_DTYPE)),
        compiler_params=_params("parallel"), name="mla_proj",
    )(x2, g.reshape(1, D), winT, wuqT, wukvT, cosT, sinT,
      col(q_lat_norm), col(kv_lat_norm), col(qn_norm), col(qr_norm), col(kn_norm), col(kr_norm))


def _mla_attn_kernel(q_ref, k_ref, v_ref, o_ref, s_scr, *, tq, tk):
    qi = pl.program_id(2)
    unroll = tq // tk
    nsub, _, tl = q_ref.shape
    qT = jnp.concatenate([q_ref[s] for s in range(nsub)], axis=1)
    diff = lax.broadcasted_iota(jnp.int32, (tk, 1), 0) - lax.broadcasted_iota(jnp.int32, (1, tq), 1)

    def scores(kt):
        return _dot(k_ref[pl.ds(pl.multiple_of(kt * tk, tk), tk), :], qT)

    _, l, acc = _flash_causal(qi, unroll, tk, scores, lambda kt: v_ref[kt], diff, s_scr,
                              _flash_init(MLA_V, tq))
    o = (acc / l).astype(o_ref.dtype)
    for s in range(nsub):
        o_ref[s] = o[:, s * tl:(s + 1) * tl]


def _mla_attn(qT, k, vT, B, S):
    tq, tk = MLA_TQ, MLA_TK
    tl = qT.shape[2]
    nsub = tq // tl
    nq = S // tq
    nkt = S // tk
    T = B * S
    return pl.pallas_call(
        functools.partial(_mla_attn_kernel, tq=tq, tk=tk),
        grid=(B, MLA_HEADS, nq),
        in_specs=[pl.BlockSpec((nsub, MLA_QK, tl), lambda b, h, i: (b * nq + i, h, 0)),
                  pl.BlockSpec((None, None, S, MLA_QK), lambda b, h, i: (b, h, 0, 0)),
                  pl.BlockSpec((nkt, MLA_V, tk), lambda b, h, i: (b, h, 0))],
        out_specs=pl.BlockSpec((nsub, MLA_V, tl), lambda b, h, i: (b * nq + i, h, 0)),
        out_shape=jax.ShapeDtypeStruct((T // tl, MLA_HEADS * MLA_V, tl), MXU_DTYPE),
        scratch_shapes=[pltpu.VMEM((tq // tk, tk, tq), F32)],
        compiler_params=_params("parallel", "parallel", "arbitrary"), name="mla_attn",
    )(qT, k, vT)


def _moe_router_kernel(x_ref, g_ref, w_ref, b_ref, id_ref, wt_ref):
    tm = x_ref.shape[0]
    h = _rms_lanes(x_ref[...], g_ref[...]).astype(MXU_DTYPE)
    lg = _dot_nt(w_ref[...], h) + b_ref[...]
    gl = lg[:MOE_GROUPS]
    ge = jnp.exp(gl - jnp.max(gl, axis=0, keepdims=True))
    gp = ge / jnp.sum(ge, axis=0, keepdims=True)
    rid = lax.broadcasted_iota(jnp.int32, (MOE_GROUPS, 1), 0)
    g_w = jnp.max(gp, axis=0, keepdims=True)
    grp = jnp.min(jnp.where(gp == g_w, rid, MOE_GROUPS), axis=0, keepdims=True)
    el = jnp.zeros((MOE_EPG, tm), F32)
    for gi in range(MOE_GROUPS):
        rows = lg[MOE_GROUPS + gi * MOE_EPG:MOE_GROUPS + (gi + 1) * MOE_EPG]
        el = jnp.where(grp == gi, rows, el)
    ee = jnp.exp(el - jnp.max(el, axis=0, keepdims=True))
    ep = ee / jnp.sum(ee, axis=0, keepdims=True)
    p1 = jnp.max(ep, axis=0, keepdims=True)
    i1 = jnp.min(jnp.where(ep == p1, rid, MOE_EPG), axis=0, keepdims=True)
    ep2 = jnp.where(rid == i1, -1.0, ep)
    p2 = jnp.max(ep2, axis=0, keepdims=True)
    i2 = jnp.min(jnp.where(ep2 == p2, rid, MOE_EPG), axis=0, keepdims=True)
    den = p1 + p2
    zi = jnp.zeros((1, tm), jnp.int32)
    zf = jnp.zeros((1, tm), F32)
    id_ref[...] = jnp.concatenate([grp * MOE_EPG + i1, grp * MOE_EPG + i2] + [zi] * 6, axis=0)
    wt_ref[...] = jnp.concatenate([g_w * p1 / den, g_w * p2 / den] + [zf] * 6, axis=0)


def _moe_router(x2, g, w_group, b_group, w_expert, b_expert):
    T, D = x2.shape
    tm = ROW_TILE
    nrow = 128
    pad = nrow - MOE_GROUPS - MOE_EXPERTS
    wT = jnp.concatenate([w_group, w_expert, jnp.zeros((D, pad), w_group.dtype)], axis=1).T.astype(MXU_DTYPE)
    bias = jnp.concatenate([b_group, b_expert, jnp.zeros((pad,), b_group.dtype)]).reshape(nrow, 1)
    ids, wts = pl.pallas_call(
        _moe_router_kernel,
        grid=(T // tm,),
        in_specs=[pl.BlockSpec((tm, D), lambda i: (i, 0)), pl.BlockSpec((1, D), lambda i: (0, 0)),
                  pl.BlockSpec((nrow, D), lambda i: (0, 0)), pl.BlockSpec((nrow, 1), lambda i: (0, 0))],
        out_specs=(pl.BlockSpec((None, 8, tm), lambda i: (i, 0, 0)), pl.BlockSpec((None, 8, tm), lambda i: (i, 0, 0))),
        out_shape=(jax.ShapeDtypeStruct((T // tm, 8, tm), jnp.int32), jax.ShapeDtypeStruct((T // tm, 8, tm), F32)),
        compiler_params=_params("parallel"), name="moe_router",
    )(x2, g.reshape(1, D), wT, bias)
    ids = ids[:, :MOE_TOP_K, :].transpose(0, 2, 1).reshape(T, MOE_TOP_K)
    wts = wts[:, :MOE_TOP_K, :].transpose(0, 2, 1).reshape(T, MOE_TOP_K)
    return ids, wts


def _moe_expert_kernel(be_ref, nu_ref, tok_ref, x_hbm, g_ref, sw_ref, wg_ref, wu_ref, wd_ref, y_ref, xbuf, sem):
    i = pl.program_id(0)
    nrows = xbuf.shape[0]

    def row_copy(r):
        return pltpu.make_async_copy(x_hbm.at[pl.ds(tok_ref[0, 0, r], 1), :], xbuf.at[pl.ds(r, 1), :], sem.at[0])

    @pl.when(i < nu_ref[0])
    def _():
        def start(r, c):
            row_copy(r).start()
            return c

        def wait(r, c):
            row_copy(r).wait()
            return c

        lax.fori_loop(0, nrows, start, 0)
        lax.fori_loop(0, nrows, wait, 0)
        h = _rms_lanes(xbuf[...], g_ref[...]).astype(MXU_DTYPE)
        a = _dot(h, wg_ref[...])
        u = _dot(h, wu_ref[...])
        mid = (jax.nn.silu(a) * u).astype(MXU_DTYPE)
        y_ref[...] = _dot(mid, wd_ref[...]) * sw_ref[...]

    @pl.when(i >= nu_ref[0])
    def _():
        y_ref[...] = jnp.zeros_like(y_ref)


def _moe_experts(x2, g, block_e, n_used, slot_tok, slot_w, w_gate, w_up, w_down):
    T, D = x2.shape
    blk = MOE_BLOCK
    n_blocks = block_e.shape[0]
    grid_spec = pltpu.PrefetchScalarGridSpec(
        num_scalar_prefetch=2,
        grid=(n_blocks,),
        in_specs=[
            pl.BlockSpec((1, 1, blk), lambda i, be, nu: (i, 0, 0), memory_space=pltpu.SMEM),
            pl.BlockSpec(memory_space=pl.ANY),
            pl.BlockSpec((1, D), lambda i, be, nu: (0, 0)),
            pl.BlockSpec((blk, 1), lambda i, be, nu: (i, 0)),
            pl.BlockSpec((None, D, MOE_FF), lambda i, be, nu: (be[i], 0, 0)),
            pl.BlockSpec((None, D, MOE_FF), lambda i, be, nu: (be[i], 0, 0)),
            pl.BlockSpec((None, MOE_FF, D), lambda i, be, nu: (be[i], 0, 0)),
        ],
        out_specs=pl.BlockSpec((blk, D), lambda i, be, nu: (i, 0)),
        scratch_shapes=[pltpu.VMEM((blk, D), F32), pltpu.SemaphoreType.DMA((1,))],
    )
    return pl.pallas_call(
        _moe_expert_kernel, grid_spec=grid_spec,
        out_shape=jax.ShapeDtypeStruct((n_blocks * blk, D), F32),
        compiler_params=_params("arbitrary"), name="moe_experts",
    )(block_e, n_used, slot_tok.reshape(n_blocks, 1, blk), x2, g.reshape(1, D), slot_w.reshape(n_blocks * blk, 1),
      w_gate.astype(MXU_DTYPE), w_up.astype(MXU_DTYPE), w_down.astype(MXU_DTYPE))


def _moe_combine_kernel(d_ref, y_hbm, x_ref, o_ref, ybuf, sem):
    tc = x_ref.shape[0]

    def row_copy(r):
        return pltpu.make_async_copy(y_hbm.at[pl.ds(d_ref[0, 0, r], 1), :], ybuf.at[pl.ds(r, 1), :], sem.at[0])

    def start(r, c):
        row_copy(r).start()
        return c

    def wait(r, c):
        row_copy(r).wait()
        return c

    lax.fori_loop(0, MOE_TOP_K * tc, start, 0)
    lax.fori_loop(0, MOE_TOP_K * tc, wait, 0)
    o_ref[...] = x_ref[...] + ybuf[:tc, :] + ybuf[tc:, :]


def _moe_combine(x2, y, dest):
    T, D = x2.shape
    tc = MOE_TC
    d3 = dest.reshape(T // tc, tc, MOE_TOP_K).transpose(0, 2, 1).reshape(T // tc, 1, MOE_TOP_K * tc)
    return pl.pallas_call(
        _moe_combine_kernel,
        grid=(T // tc,),
        in_specs=[pl.BlockSpec((1, 1, MOE_TOP_K * tc), lambda i: (i, 0, 0), memory_space=pltpu.SMEM),
                  pl.BlockSpec(memory_space=pl.ANY),
                  pl.BlockSpec((tc, D), lambda i: (i, 0))],
        out_specs=pl.BlockSpec((tc, D), lambda i: (i, 0)),
        out_shape=jax.ShapeDtypeStruct((T, D), F32),
        scratch_shapes=[pltpu.VMEM((MOE_TOP_K * tc, D), F32), pltpu.SemaphoreType.DMA((1,))],
        compiler_params=_params("arbitrary"), name="moe_combine",
    )(d3, y, x2)


def _moe_dispatch(ids, wts, T):
    n_assign = T * MOE_TOP_K
    flat_e = ids.reshape(n_assign)
    flat_tok = jnp.arange(n_assign, dtype=jnp.int32) // MOE_TOP_K
    order = jnp.argsort(flat_e)
    sorted_e = flat_e[order]
    counts = jnp.bincount(flat_e, length=MOE_EXPERTS)
    starts = jnp.cumsum(counts) - counts
    padded = (counts + MOE_BLOCK - 1) // MOE_BLOCK * MOE_BLOCK
    pad_ends = jnp.cumsum(padded)
    pad_starts = pad_ends - padded
    dest_sorted = (pad_starts[sorted_e] + jnp.arange(n_assign) - starts[sorted_e]).astype(jnp.int32)
    n_blocks = -(-n_assign // MOE_BLOCK) + MOE_EXPERTS
    n_slots = n_blocks * MOE_BLOCK
    slot_tok = jnp.zeros((n_slots,), jnp.int32).at[dest_sorted].set(flat_tok[order])
    slot_w = jnp.zeros((n_slots,), F32).at[dest_sorted].set(wts.reshape(n_assign)[order])
    block_e = jnp.minimum(jnp.searchsorted(pad_ends, jnp.arange(n_blocks) * MOE_BLOCK, side='right'),
                          MOE_EXPERTS - 1).astype(jnp.int32)
    dest = jnp.zeros((n_assign,), jnp.int32).at[order].set(dest_sorted).reshape(T, MOE_TOP_K)
    n_used = (pad_ends[-1] // MOE_BLOCK).astype(jnp.int32).reshape(1)
    return block_e, n_used, slot_tok, slot_w, dest


def _hier_moe(x2, g, w_group, b_group, w_expert, b_expert, w_gate, w_up, w_down):
    T = x2.shape[0]
    ids, wts = _moe_router(x2, g, w_group, b_group, w_expert, b_expert)
    block_e, n_used, slot_tok, slot_w, dest = _moe_dispatch(ids, wts, T)
    y = _moe_experts(x2, g, block_e, n_used, slot_tok, slot_w, w_gate, w_up, w_down)
    return _moe_combine(x2, y, dest)


def _rope_tables_T(seq, dim):
    inv = 1.0 / (ROPE_THETA ** (jnp.arange(0, dim, 2, dtype=F32) / dim))
    ang = jnp.arange(seq, dtype=F32)[:, None] * inv[None, :]
    return jnp.cos(ang).T, jnp.sin(ang).T


def _nsa_layer(x2, g, w_in, cmp_pos, cmp_w1, cmp_w2, q_norm, k_norm, w_out, cosT, sinT, B, S):
    qn, qr, ks, kw, vsT, vwT, gT, kc, vc = _nsa_proj(x2, g, w_in, q_norm, k_norm, cosT, sinT, B, S)
    kcmp = _compress(kc, cmp_pos[0], cmp_w1[0], cmp_w2[0], k_norm[0], B, S, True)
    vcmpT = _compress(vc, cmp_pos[1], cmp_w1[1], cmp_w2[1], k_norm[0], B, S, False)
    T = B * S
    gates5 = gT.reshape(T // NSA_TQ, 3, NSA_GROUPS, NSA_REP, NSA_TQ)
    ocg, selb = _nsa_cmp(qn, kcmp, vcmpT, gates5, B, S)
    aT = _nsa_attn(qr, ks, vsT, kw, vwT, selb, ocg, gates5, B, S)
    return _out_proj(aT, x2, w_out)


def _mla_layer(x2, g, w_in, q_lat_norm, kv_lat_norm, w_uq, w_ukv, qn_norm, qr_norm, kn_norm, kr_norm,
               w_out, cosT, sinT, B, S):
    qT, k, vT = _mla_proj(x2, g, w_in, q_lat_norm, kv_lat_norm, w_uq, w_ukv, qn_norm, qr_norm, kn_norm, kr_norm,
                          cosT, sinT, B, S)
    aT = _mla_attn(qT, k, vT, B, S)
    return _out_proj(aT, x2, w_out)


def kernel(x, norm_mix, norm_ffn, nsa_w_in, nsa_cmp_pos, nsa_cmp_w1, nsa_cmp_w2, nsa_q_norm, nsa_k_norm, nsa_w_out, mla_w_in, mla_q_lat_norm, mla_kv_lat_norm, mla_w_uq, mla_w_ukv, mla_q_nope_norm, mla_q_rope_norm, mla_k_nope_norm, mla_k_rope_norm, mla_w_out, moe_w_group, moe_b_group, moe_w_expert, moe_b_expert, moe_w_gate, moe_w_up, moe_w_down):
    B, S, D = x.shape
    depth = norm_mix.shape[0]
    assert S % max(ROW_TILE, NSA_TQ, MLA_TQ) == 0 and MLA_TQ % ROW_TILE == 0 and S >= WINDOW
    cosT, sinT = _rope_tables_T(S, NSA_HEAD_DIM)
    x2 = x.reshape(B * S, D)
    for i in range(depth):
        j = i // 2
        if i % 2 == 0:
            x2 = _nsa_layer(x2, norm_mix[i], nsa_w_in[j], nsa_cmp_pos[j], nsa_cmp_w1[j], nsa_cmp_w2[j],
                            nsa_q_norm[j], nsa_k_norm[j], nsa_w_out[j], cosT, sinT, B, S)
        else:
            x2 = _mla_layer(x2, norm_mix[i], mla_w_in[j], mla_q_lat_norm[j], mla_kv_lat_norm[j], mla_w_uq[j],
                            mla_w_ukv[j], mla_q_nope_norm[j], mla_q_rope_norm[j], mla_k_nope_norm[j],
                            mla_k_rope_norm[j], mla_w_out[j], cosT, sinT, B, S)
        x2 = _hier_moe(x2, norm_ffn[i], moe_w_group[i], moe_b_group[i], moe_w_expert[i], moe_b_expert[i],
                       moe_w_gate[i], moe_w_up[i], moe_w_down[i])
    return x2.reshape(B, S, D)
```

```python
import functools

import numpy as np
import jax
import jax.numpy as jnp
from jax import lax
from jax.experimental import pallas as pl
from jax.experimental.pallas import tpu as pltpu

RMS_EPS = 1e-6
ROPE_THETA = 10000.0
NEG_INF = -1e30

NSA_HEADS = 16
NSA_HEAD_DIM = 64
NSA_GROUPS = 4
NSA_REP = NSA_HEADS // NSA_GROUPS
CMP_LEN = 32
CMP_STRIDE = 16
CMP_HIDDEN = 128
SEL_LEN = 64
SEL_RATIO = SEL_LEN // CMP_STRIDE
SEL_TOPK = 16
SEL_FORCE = 1e4
WINDOW = 512
NSA_Q_DIM = NSA_HEADS * NSA_HEAD_DIM
NSA_KV_DIM = NSA_GROUPS * NSA_HEAD_DIM

MLA_HEADS = 8
MLA_NOPE = 128
MLA_ROPE = 64
MLA_V = 128
MLA_QK = MLA_NOPE + MLA_ROPE
MLA_Q_LORA = 256
MLA_KV_LORA = 256

MOE_GROUPS = 8
MOE_EPG = 8
MOE_EXPERTS = MOE_GROUPS * MOE_EPG
MOE_TOP_K = 2
MOE_FF = 256

MXU_DTYPE = jnp.bfloat16
ROW_TILE = 512
NSA_TQ = 512
NSA_TK = 256
MLA_TQ = 1024
MLA_TK = 256
MOE_BLOCK = 128
MOE_TC = 256
VMEM_LIMIT = 56 * 1024 * 1024

_NT = (((1,), (1,)), ((), ()))
F32 = jnp.float32
LOG2E = 1.4426950408889634


def _params(*sem):
    return pltpu.CompilerParams(dimension_semantics=sem, vmem_limit_bytes=VMEM_LIMIT)


def _dot(a, b):
    return jnp.dot(a, b, preferred_element_type=F32)


def _dot_nt(a, b):
    return lax.dot_general(a, b, _NT, preferred_element_type=F32)


def _rms_lanes(x, g):
    ms = jnp.mean(x * x, axis=-1, keepdims=True)
    return x * lax.rsqrt(ms + RMS_EPS) * g


def _rms_rows(y, g):
    ms = jnp.mean(y * y, axis=-2, keepdims=True)
    return y * lax.rsqrt(ms + RMS_EPS) * g


def _rope_rows(y, cos, sin):
    half = y.shape[-2] // 2
    y1 = y[..., :half, :]
    y2 = y[..., half:, :]
    return jnp.concatenate([y1 * cos - y2 * sin, y2 * cos + y1 * sin], axis=-2)


def _nsa_proj_kernel(x_ref, g_ref, wq_ref, wk_ref, wv_ref, wg_ref, wc_ref, cos_ref, sin_ref,
                     qg_ref, kg_ref, blk_ref,
                     qn_ref, qr_ref, ks_ref, kw_ref, vs_ref, vw_ref, gt_ref, kc_ref, vc_ref,
                     *, tq, tk):
    tm = x_ref.shape[0]
    h = _rms_lanes(x_ref[...], g_ref[...]).astype(MXU_DTYPE)
    cos = cos_ref[...]
    sin = sin_ref[...]
    scale = NSA_HEAD_DIM ** -0.5 * LOG2E

    q3 = _dot_nt(wq_ref[...], h).reshape(NSA_HEADS, NSA_HEAD_DIM, tm)
    qn = _rms_rows(q3, qg_ref[...])
    qr = _rope_rows(qn, cos, sin)
    qn = (qn * scale).astype(MXU_DTYPE).reshape(NSA_Q_DIM, tm)
    qr = (qr * scale).astype(MXU_DTYPE).reshape(NSA_Q_DIM, tm)
    for s in range(tm // tq):
        qn_ref[s] = qn[:, s * tq:(s + 1) * tq]
        qr_ref[s] = qr[:, s * tq:(s + 1) * tq]

    k4 = _dot_nt(wk_ref[...], h).reshape(2, NSA_GROUPS, NSA_HEAD_DIM, tm)
    for br, o_ref in ((0, ks_ref), (1, kw_ref)):
        kb = _rope_rows(_rms_rows(k4[br], kg_ref[br]), cos, sin)
        kb = kb.reshape(NSA_KV_DIM, tm).T.astype(MXU_DTYPE)
        for g in range(NSA_GROUPS):
            kg = kb[:, g * NSA_HEAD_DIM:(g + 1) * NSA_HEAD_DIM]
            o_ref[g] = jnp.concatenate([kg, blk_ref[...]], axis=1) if br == 0 else kg

    vT = _dot_nt(wv_ref[...], h).astype(MXU_DTYPE)
    for s in range(tm // tk):
        vs_ref[s] = vT[:NSA_KV_DIM, s * tk:(s + 1) * tk]
        vw_ref[s] = vT[NSA_KV_DIM:, s * tk:(s + 1) * tk]

    gT = jax.nn.sigmoid(_dot_nt(wg_ref[...], h))
    for s in range(tm // tq):
        gt_ref[s] = gT[:, s * tq:(s + 1) * tq]

    c = _dot(h, wc_ref[...])
    kc_ref[...] = c[:, :NSA_KV_DIM]
    vc_ref[...] = c[:, NSA_KV_DIM:]


def _nsa_proj(x2, g, w_in, q_norm, k_norm, cosT, sinT, B, S):
    T, D = x2.shape
    tm, tq, tk = ROW_TILE, NSA_TQ, NSA_TK
    nps = S // tm
    o = NSA_Q_DIM
    kvd = NSA_KV_DIM
    wq = w_in[:, :o].T.astype(MXU_DTYPE)
    wc = w_in[:, o:o + 2 * kvd].astype(MXU_DTYPE)
    wk = jnp.concatenate([w_in[:, o + 2 * kvd:o + 3 * kvd], w_in[:, o + 4 * kvd:o + 5 * kvd]], axis=1).T.astype(MXU_DTYPE)
    wv = jnp.concatenate([w_in[:, o + 3 * kvd:o + 4 * kvd], w_in[:, o + 5 * kvd:o + 6 * kvd]], axis=1).T.astype(MXU_DTYPE)
    wg = w_in[:, o + 6 * kvd:].T.astype(MXU_DTYPE)
    ng = wg.shape[0]
    nsel = S // SEL_LEN
    assert NSA_HEAD_DIM + nsel <= 256
    blk_onehot = jnp.asarray(np.arange(S)[:, None] // SEL_LEN == np.arange(nsel)[None, :], MXU_DTYPE)
    full = lambda shape: pl.BlockSpec(shape, lambda i: (0,) * len(shape))
    out_shapes = (
        jax.ShapeDtypeStruct((T // tq, o, tq), MXU_DTYPE),
        jax.ShapeDtypeStruct((T // tq, o, tq), MXU_DTYPE),
        jax.ShapeDtypeStruct((B, NSA_GROUPS, S, NSA_HEAD_DIM + nsel), MXU_DTYPE),
        jax.ShapeDtypeStruct((B, NSA_GROUPS, S, NSA_HEAD_DIM), MXU_DTYPE),
        jax.ShapeDtypeStruct((T // tk, kvd, tk), MXU_DTYPE),
        jax.ShapeDtypeStruct((T // tk, kvd, tk), MXU_DTYPE),
        jax.ShapeDtypeStruct((T // tq, ng, tq), F32),
        jax.ShapeDtypeStruct((T, kvd), F32),
        jax.ShapeDtypeStruct((T, kvd), F32),
    )
    kspec = pl.BlockSpec((None, NSA_GROUPS, tm, NSA_HEAD_DIM), lambda i: (i // nps, 0, i % nps, 0))
    kaspec = pl.BlockSpec((None, NSA_GROUPS, tm, NSA_HEAD_DIM + nsel), lambda i: (i // nps, 0, i % nps, 0))
    out_specs = (
        pl.BlockSpec((tm // tq, o, tq), lambda i: (i, 0, 0)),
        pl.BlockSpec((tm // tq, o, tq), lambda i: (i, 0, 0)),
        kaspec, kspec,
        pl.BlockSpec((tm // tk, kvd, tk), lambda i: (i, 0, 0)),
        pl.BlockSpec((tm // tk, kvd, tk), lambda i: (i, 0, 0)),
        pl.BlockSpec((tm // tq, ng, tq), lambda i: (i, 0, 0)),
        pl.BlockSpec((tm, kvd), lambda i: (i, 0)),
        pl.BlockSpec((tm, kvd), lambda i: (i, 0)),
    )
    in_specs = [
        pl.BlockSpec((tm, D), lambda i: (i, 0)),
        full((1, D)), full(wq.shape), full(wk.shape), full(wv.shape), full(wg.shape), full(wc.shape),
        pl.BlockSpec((NSA_HEAD_DIM // 2, tm), lambda i: (0, i % nps)),
        pl.BlockSpec((NSA_HEAD_DIM // 2, tm), lambda i: (0, i % nps)),
        full((NSA_HEAD_DIM, 1)), full((2, NSA_HEAD_DIM, 1)),
        pl.BlockSpec((tm, nsel), lambda i: (i % nps, 0)),
    ]
    return pl.pallas_call(
        functools.partial(_nsa_proj_kernel, tq=tq, tk=tk),
        grid=(T // tm,), in_specs=in_specs, out_specs=out_specs, out_shape=out_shapes,
        compiler_params=_params("parallel"), name="nsa_proj",
    )(x2, g.reshape(1, D), wq, wk, wv, wg, wc, cosT, sinT,
      q_norm.reshape(NSA_HEAD_DIM, 1), k_norm[1:3].reshape(2, NSA_HEAD_DIM, 1), blk_onehot)


def _compress_kernel(x_ref, pa_ref, pb_ref, w1a_ref, w1b_ref, w2t_ref, gain_ref, o_ref, *, normalize, kchunk):
    n, kdim = x_ref.shape
    hdim = w1a_ref.shape[1]
    h1 = jnp.zeros((n, hdim), F32)
    h2 = jnp.zeros((n, hdim), F32)
    for c in range(kdim // kchunk):
        sl = slice(c * kchunk, (c + 1) * kchunk)
        xs = x_ref[:, sl]
        h1 = h1 + _dot((xs + pa_ref[:, sl]).astype(MXU_DTYPE), w1a_ref[sl, :])
        h2 = h2 + _dot((xs + pb_ref[:, sl]).astype(MXU_DTYPE), w1b_ref[sl, :])
    hid = h1 + pltpu.roll(h2, n - 1, axis=0)
    act = jax.nn.gelu(hid).astype(MXU_DTYPE)
    oT = _dot_nt(w2t_ref[...], act)
    if normalize:
        o3 = _rms_rows(oT.reshape(NSA_GROUPS, NSA_HEAD_DIM, n), gain_ref[...])
        o = o3.reshape(NSA_KV_DIM, n).T.astype(MXU_DTYPE)
        for g in range(NSA_GROUPS):
            o_ref[g] = o[:, g * NSA_HEAD_DIM:(g + 1) * NSA_HEAD_DIM]
    else:
        o_ref[...] = oT.astype(MXU_DTYPE)


def _compress_weights(pos, w1, w2):
    G, Dh, H = NSA_GROUPS, NSA_HEAD_DIM, CMP_HIDDEN
    eye = jnp.eye(G, dtype=w1.dtype)

    def expand(w1_half):
        w = w1_half.reshape(CMP_STRIDE, Dh, H)
        return jnp.einsum('rdj,gk->rgdkj', w, eye).reshape(CMP_STRIDE * G * Dh, G * H).astype(MXU_DTYPE)

    half = CMP_STRIDE * Dh
    w1a, w1b = expand(w1[:half]), expand(w1[half:])
    tilepos = lambda p: jnp.broadcast_to(p[:, None, :], (CMP_STRIDE, G, Dh)).reshape(1, CMP_STRIDE * G * Dh)
    pa, pb = tilepos(pos[:CMP_STRIDE]), tilepos(pos[CMP_STRIDE:])
    w2t = jnp.einsum('jd,gk->kdgj', w2, eye).reshape(G * Dh, G * H).astype(MXU_DTYPE)
    return pa, pb, w1a, w1b, w2t


def _compress(xc, pos, w1, w2, gain, B, S, normalize):
    n = S // CMP_STRIDE
    kdim = CMP_STRIDE * NSA_KV_DIM
    xr = xc.reshape(B * n, kdim)
    pa, pb, w1a, w1b, w2t = _compress_weights(pos, w1, w2)
    full = lambda shape: pl.BlockSpec(shape, lambda b: (0,) * len(shape))
    if normalize:
        out_shape = jax.ShapeDtypeStruct((B, NSA_GROUPS, n, NSA_HEAD_DIM), MXU_DTYPE)
        out_spec = pl.BlockSpec((None, NSA_GROUPS, n, NSA_HEAD_DIM), lambda b: (b, 0, 0, 0))
    else:
        out_shape = jax.ShapeDtypeStruct((B, NSA_KV_DIM, n), MXU_DTYPE)
        out_spec = pl.BlockSpec((None, NSA_KV_DIM, n), lambda b: (b, 0, 0))
    return pl.pallas_call(
        functools.partial(_compress_kernel, normalize=normalize, kchunk=1024),
        grid=(B,),
        in_specs=[pl.BlockSpec((n, kdim), lambda b: (b, 0)), full(pa.shape), full(pb.shape),
                  full(w1a.shape), full(w1b.shape), full(w2t.shape), full((NSA_HEAD_DIM, 1))],
        out_specs=out_spec, out_shape=out_shape,
        compiler_params=_params("parallel"), name="nsa_compress",
    )(xr, pa, pb, w1a, w1b, w2t, gain.reshape(NSA_HEAD_DIM, 1))


def _stack_heads(ref):
    return jnp.concatenate([ref[r * NSA_HEAD_DIM:(r + 1) * NSA_HEAD_DIM, :] for r in range(NSA_REP)], axis=1)


def _nsa_cmp_kernel(qn_ref, kc_ref, vct_ref, wimp_ref, gt_ref, oc_ref, sel_ref, *, tq, sel_k):
    qi = pl.program_id(2)
    q0 = qi * tq
    ncmp = kc_ref.shape[0]
    nsel = sel_ref.shape[0]
    lanes = NSA_REP * tq
    qT = _stack_heads(qn_ref)
    s = _dot(kc_ref[...], qT)
    t4 = q0 + lax.broadcasted_iota(jnp.int32, (1, lanes), 1) % tq
    cmp_end = lax.broadcasted_iota(jnp.int32, (ncmp, 1), 0) * CMP_STRIDE + (CMP_LEN - 1)
    mask = cmp_end <= t4
    s = jnp.where(mask, s, NEG_INF)
    m = jnp.max(s, axis=0, keepdims=True)
    e = jnp.where(mask, jnp.exp2(s - m), 0.0)
    p = e / jnp.maximum(jnp.sum(e, axis=0, keepdims=True), 1e-30)
    ocT = _dot(vct_ref[...], p.astype(MXU_DTYPE))
    for r in range(NSA_REP):
        oc_ref[r * NSA_HEAD_DIM:(r + 1) * NSA_HEAD_DIM, :] = ocT[:, r * tq:(r + 1) * tq] * gt_ref[0, r:r + 1, :]

    psum = p[:, :tq]
    for r in range(1, NSA_REP):
        psum = psum + p[:, r * tq:(r + 1) * tq]
    hi = psum.astype(MXU_DTYPE)
    lo = (psum - hi.astype(F32)).astype(MXU_DTYPE)
    imp = _dot(wimp_ref[...], hi) + _dot(wimp_ref[...], lo)

    t = q0 + lax.broadcasted_iota(jnp.int32, (1, tq), 1)
    j = lax.broadcasted_iota(jnp.int32, (nsel, 1), 0)
    cur = t // SEL_LEN
    valid = j * SEL_LEN <= t
    forced = (j == 0) | (j == cur) | (j == cur - 1)
    score = jnp.where(forced, SEL_FORCE, jnp.where(valid, imp, -1.0))
    jf = j.astype(F32)
    sel = jnp.zeros((nsel, tq), jnp.bool_)
    for _ in range(sel_k):
        mx = jnp.max(score, axis=0, keepdims=True)
        first = jnp.min(jnp.where(score == mx, jf, float(nsel)), axis=0, keepdims=True)
        hit = jf == first
        sel = sel | hit
        score = jnp.where(hit, -jnp.inf, score)
    sel_ref[...] = jnp.where(sel, 0.0, NEG_INF).astype(sel_ref.dtype)


def _imp_weights(S):
    nsel, ncmp_pad = S // SEL_LEN, S // CMP_STRIDE
    ncmp = ncmp_pad - (CMP_LEN // CMP_STRIDE) + 1
    w = np.zeros((nsel, ncmp_pad), np.float32)
    for jj in range(nsel):
        for off in range(-(CMP_LEN // CMP_STRIDE - 1), SEL_RATIO):
            i = SEL_RATIO * jj + off
            lo = max(off * CMP_STRIDE, 0)
            hi = min(off * CMP_STRIDE + CMP_LEN - 1, SEL_LEN - 1)
            if 0 <= i < ncmp:
                w[jj, i] = (hi - lo + 1) / CMP_STRIDE
    return jnp.asarray(w, MXU_DTYPE)


def _nsa_cmp(qn, kcmp, vcmpT, gates5, B, S):
    tq = NSA_TQ
    nq = S // tq
    ncmp = S // CMP_STRIDE
    nsel = S // SEL_LEN
    wimp = _imp_weights(S)
    T = B * S
    qdim = NSA_REP * NSA_HEAD_DIM
    return pl.pallas_call(
        functools.partial(_nsa_cmp_kernel, tq=tq, sel_k=min(SEL_TOPK, nsel)),
        grid=(B, NSA_GROUPS, nq),
        in_specs=[
            pl.BlockSpec((None, qdim, tq), lambda b, g, i: (b * nq + i, g, 0)),
            pl.BlockSpec((None, None, ncmp, NSA_HEAD_DIM), lambda b, g, i: (b, g, 0, 0)),
            pl.BlockSpec((None, NSA_HEAD_DIM, ncmp), lambda b, g, i: (b, g, 0)),
            pl.BlockSpec((nsel, ncmp), lambda b, g, i: (0, 0)),
            pl.BlockSpec((None, 3, None, NSA_REP, tq), lambda b, g, i: (b * nq + i, 0, g, 0, 0)),
        ],
        out_specs=(
            pl.BlockSpec((None, qdim, tq), lambda b, g, i: (b * nq + i, g, 0)),
            pl.BlockSpec((None, None, None, nsel, tq), lambda b, g, i: (b, g, i, 0, 0)),
        ),
        out_shape=(
            jax.ShapeDtypeStruct((T // tq, NSA_Q_DIM, tq), F32),
            jax.ShapeDtypeStruct((B, NSA_GROUPS, nq, nsel, tq), MXU_DTYPE),
        ),
        compiler_params=_params("parallel", "parallel", "parallel"), name="nsa_cmp_select",
    )(qn, kcmp, vcmpT, wimp, gates5)


def _flash_step(carry, s, vT):
    m, l, acc = carry
    m_new = jnp.maximum(m, jnp.max(s, axis=0, keepdims=True))
    alpha = jnp.exp2(m - m_new)
    p = jnp.exp2(s - m_new)
    l = alpha * l + jnp.sum(p, axis=0, keepdims=True)
    acc = alpha * acc + _dot(vT, p.astype(MXU_DTYPE))
    return m_new, l, acc


def _flash_init(dv, nq):
    return (jnp.full((1, nq), NEG_INF, F32), jnp.zeros((1, nq), F32), jnp.zeros((dv, nq), F32))


def _flash_causal(qi, unroll, tk, scores, values, diff, s_scr, carry):
    for u in range(unroll):
        s_scr[u] = scores(u)

    def body(i, carry):
        for u in range(unroll):
            m, l, acc = carry
            s = s_scr[u]
            m_new = jnp.maximum(m, jnp.max(s, axis=0, keepdims=True))
            alpha = jnp.exp2(m - m_new)
            p = jnp.exp2(s - m_new)
            l = alpha * l + jnp.sum(p, axis=0, keepdims=True)
            s_scr[u] = scores((i + 1) * unroll + u)
            acc = alpha * acc + _dot(values(i * unroll + u), p.astype(MXU_DTYPE))
            carry = (m_new, l, acc)
        return carry

    carry = lax.fori_loop(0, qi, body, carry)
    for d in range(unroll):
        s = jnp.where(diff <= -d * tk, s_scr[d], NEG_INF)
        carry = _flash_step(carry, s, values(qi * unroll + d))
    return carry


def _nsa_attn_kernel(qr_ref, ks_ref, vs_ref, kw_ref, vw_ref, sel_ref, oc_ref, gt_ref, o_ref, s_scr, *, tq, tk):
    qi = pl.program_id(2)
    lanes = NSA_REP * tq
    dh = NSA_HEAD_DIM
    unroll = tq // tk
    qT = _stack_heads(qr_ref)
    q_aug = jnp.concatenate([qT, jnp.concatenate([sel_ref[...]] * NSA_REP, axis=1)], axis=0)
    tl = lax.broadcasted_iota(jnp.int32, (1, lanes), 1) % tq
    diff = lax.broadcasted_iota(jnp.int32, (tk, 1), 0) - tl

    def sel_scores(kt):
        return _dot(ks_ref[pl.ds(pl.multiple_of(kt * tk, tk), tk), :], q_aug)

    _, l_s, acc_s = _flash_causal(qi, unroll, tk, sel_scores, lambda kt: vs_ref[kt], diff, s_scr,
                                  _flash_init(dh, lanes))

    def win_scores(d):
        kt = qi * unroll + d
        mask = diff <= -d * tk if d >= 0 else None
        lo_thr = -d * tk - WINDOW
        if d < 0:
            lo = diff > jnp.where(kt >= 0, lo_thr, 2 ** 30)
            kt = jnp.maximum(kt, 0)
        else:
            lo = diff > lo_thr if lo_thr >= -(tq - 1) else None
        if lo is not None:
            mask = lo if mask is None else mask & lo
        s = _dot(kw_ref[pl.ds(pl.multiple_of(kt * tk, tk), tk), :], qT)
        return s if mask is None else jnp.where(mask, s, NEG_INF)

    order = [0] + list(range(1, unroll)) + list(range(-(WINDOW // tk), 0))
    carry = _flash_init(dh, lanes)
    nxt = win_scores(order[0])
    for i, d in enumerate(order):
        m, l, acc = carry
        s = nxt
        m_new = jnp.maximum(m, jnp.max(s, axis=0, keepdims=True))
        alpha = jnp.exp2(m - m_new)
        p = jnp.exp2(s - m_new)
        l = alpha * l + jnp.sum(p, axis=0, keepdims=True)
        if i + 1 < len(order):
            nxt = win_scores(order[i + 1])
        acc = alpha * acc + _dot(vw_ref[jnp.maximum(qi * unroll + d, 0)], p.astype(MXU_DTYPE))
        carry = (m_new, l, acc)
    _, l_w, acc_w = carry

    o_s = acc_s / l_s
    o_w = acc_w / l_w
    for r in range(NSA_REP):
        ls = slice(r * tq, (r + 1) * tq)
        rs = slice(r * dh, (r + 1) * dh)
        o = oc_ref[rs, :] + gt_ref[1, r:r + 1, :] * o_s[:, ls] + gt_ref[2, r:r + 1, :] * o_w[:, ls]
        o_ref[rs, :] = o.astype(o_ref.dtype)


def _nsa_attn(qr, ks, vsT, kw, vwT, selb, ocg, gates5, B, S):
    tq, tk = NSA_TQ, NSA_TK
    nq = S // tq
    nkt = S // tk
    nsel = S // SEL_LEN
    T = B * S
    qdim = NSA_REP * NSA_HEAD_DIM
    assert tq % tk == 0 and WINDOW % tk == 0 and tq <= WINDOW
    kspec = pl.BlockSpec((None, None, S, NSA_HEAD_DIM), lambda b, g, i: (b, g, 0, 0))
    kaspec = pl.BlockSpec((None, None, S, NSA_HEAD_DIM + nsel), lambda b, g, i: (b, g, 0, 0))
    vspec = pl.BlockSpec((nkt, NSA_HEAD_DIM, tk), lambda b, g, i: (b, g, 0))
    qspec = pl.BlockSpec((None, qdim, tq), lambda b, g, i: (b * nq + i, g, 0))
    return pl.pallas_call(
        functools.partial(_nsa_attn_kernel, tq=tq, tk=tk),
        grid=(B, NSA_GROUPS, nq),
        in_specs=[
            qspec, kaspec, vspec, kspec, vspec,
            pl.BlockSpec((None, None, None, nsel, tq), lambda b, g, i: (b, g, i, 0, 0)),
            qspec,
            pl.BlockSpec((None, 3, None, NSA_REP, tq), lambda b, g, i: (b * nq + i, 0, g, 0, 0)),
        ],
        out_specs=qspec,
        out_shape=jax.ShapeDtypeStruct((T // tq, NSA_Q_DIM, tq), MXU_DTYPE),
        scratch_shapes=[pltpu.VMEM((tq // tk, tk, NSA_REP * tq), F32)],
        compiler_params=_params("parallel", "parallel", "arbitrary"), name="nsa_attn",
    )(qr, ks, vsT, kw, vwT, selb, ocg, gates5)


def _out_proj_kernel(a_ref, x_ref, w_ref, o_ref):
    nsub, _, tq = a_ref.shape
    for s in range(nsub):
        yT = _dot(w_ref[...], a_ref[s])
        o_ref[s * tq:(s + 1) * tq, :] = x_ref[s * tq:(s + 1) * tq, :] + yT.T


def _out_proj(aT, x2, w_out):
    T, D = x2.shape
    nt, feat, tq = aT.shape
    tm = ROW_TILE
    wT = w_out.T.astype(MXU_DTYPE)
    return pl.pallas_call(
        _out_proj_kernel,
        grid=(T // tm,),
        in_specs=[pl.BlockSpec((tm // tq, feat, tq), lambda i: (i, 0, 0)),
                  pl.BlockSpec((tm, D), lambda i: (i, 0)),
                  pl.BlockSpec((D, feat), lambda i: (0, 0))],
        out_specs=pl.BlockSpec((tm, D), lambda i: (i, 0)),
        out_shape=jax.ShapeDtypeStruct((T, D), F32),
        compiler_params=_params("parallel"), name="out_proj",
    )(aT, x2, wT)


def _mla_proj_kernel(x_ref, g_ref, win_ref, wuq_ref, wukv_ref, cos_ref, sin_ref,
                     gq_ref, gkv_ref, gqn_ref, gqr_ref, gkn_ref, gkr_ref,
                     q_ref, k_ref, v_ref, *, tq, tk):
    tm = x_ref.shape[0]
    h = _rms_lanes(x_ref[...], g_ref[...]).astype(MXU_DTYPE)
    cos = cos_ref[...]
    sin = sin_ref[...]
    cT = _dot_nt(win_ref[...], h)
    cq = _rms_rows(cT[:MLA_Q_LORA], gq_ref[...]).astype(MXU_DTYPE)
    ckv = _rms_rows(cT[MLA_Q_LORA:MLA_Q_LORA + MLA_KV_LORA], gkv_ref[...]).astype(MXU_DTYPE)
    kr = _rope_rows(_rms_rows(cT[MLA_Q_LORA + MLA_KV_LORA:], gkr_ref[...]), cos, sin)

    scale = MLA_QK ** -0.5 * LOG2E
    q3 = _dot(wuq_ref[...], cq).reshape(MLA_HEADS, MLA_QK, tm)
    qn = _rms_rows(q3[:, :MLA_NOPE, :], gqn_ref[...])
    qr = _rope_rows(_rms_rows(q3[:, MLA_NOPE:, :], gqr_ref[...]), cos, sin)
    q = (jnp.concatenate([qn, qr], axis=1) * scale).astype(MXU_DTYPE).reshape(MLA_HEADS * MLA_QK, tm)
    for s in range(tm // tq):
        q_ref[s] = q[:, s * tq:(s + 1) * tq]

    kv3 = _dot(wukv_ref[...], ckv).reshape(MLA_HEADS, MLA_NOPE + MLA_V, tm)
    vT = kv3[:, MLA_NOPE:, :].astype(MXU_DTYPE).reshape(MLA_HEADS * MLA_V, tm)
    for s in range(tm // tk):
        v_ref[s] = vT[:, s * tk:(s + 1) * tk]
    kn = _rms_rows(kv3[:, :MLA_NOPE, :], gkn_ref[...])
    kr_t = kr.T.astype(MXU_DTYPE)
    for hh in range(MLA_HEADS):
        k_ref[hh] = jnp.concatenate([kn[hh].T.astype(MXU_DTYPE), kr_t], axis=1)


def _mla_proj(x2, g, w_in, q_lat_norm, kv_lat_norm, w_uq, w_ukv, qn_norm, qr_norm, kn_norm, kr_norm,
              cosT, sinT, B, S):
    T, D = x2.shape
    tm, tq, tk = ROW_TILE, ROW_TILE, MLA_TK
    nps = S // tm
    winT = w_in.T.astype(MXU_DTYPE)
    wuqT = w_uq.T.astype(MXU_DTYPE)
    wukvT = w_ukv.T.astype(MXU_DTYPE)
    full = lambda shape: pl.BlockSpec(shape, lambda i: (0,) * len(shape))
    col = lambda v: v.reshape(-1, 1)
    return pl.pallas_call(
        functools.partial(_mla_proj_kernel, tq=tq, tk=tk),
        grid=(T // tm,),
        in_specs=[pl.BlockSpec((tm, D), lambda i: (i, 0)), full((1, D)),
                  full(winT.shape), full(wuqT.shape), full(wukvT.shape),
                  pl.BlockSpec((MLA_ROPE // 2, tm), lambda i: (0, i % nps)),
                  pl.BlockSpec((MLA_ROPE // 2, tm), lambda i: (0, i % nps)),
                  full((MLA_Q_LORA, 1)), full((MLA_KV_LORA, 1)), full((MLA_NOPE, 1)),
                  full((MLA_ROPE, 1)), full((MLA_NOPE, 1)), full((MLA_ROPE, 1))],
        out_specs=(pl.BlockSpec((tm // tq, MLA_HEADS * MLA_QK, tq), lambda i: (i, 0, 0)),
                   pl.BlockSpec((None, MLA_HEADS, tm, MLA_QK), lambda i: (i // nps, 0, i % nps, 0)),
                   pl.BlockSpec((tm // tk, MLA_HEADS * MLA_V, tk), lambda i: (i, 0, 0))),
        out_shape=(jax.ShapeDtypeStruct((T // tq, MLA_HEADS * MLA_QK, tq), MXU_DTYPE),
                   jax.ShapeDtypeStruct((B, MLA_HEADS, S, MLA_QK), MXU_DTYPE),
                   jax.ShapeDtypeStruct((T // tk, MLA_HEADS * MLA_V, tk), MXU_DTYPE)),
        compiler_params=_params("parallel"), name="mla_proj",
    )(x2, g.reshape(1, D), winT, wuqT, wukvT, cosT, sinT,
      col(q_lat_norm), col(kv_lat_norm), col(qn_norm), col(qr_norm), col(kn_norm), col(kr_norm))


def _mla_attn_kernel(q_ref, k_ref, v_ref, o_ref, s_scr, *, tq, tk):
    qi = pl.program_id(2)
    unroll = tq // tk
    nsub, _, tl = q_ref.shape
    qT = jnp.concatenate([q_ref[s] for s in range(nsub)], axis=1)
    diff = lax.broadcasted_iota(jnp.int32, (tk, 1), 0) - lax.broadcasted_iota(jnp.int32, (1, tq), 1)

    def scores(kt):
        return _dot(k_ref[pl.ds(pl.multiple_of(kt * tk, tk), tk), :], qT)

    _, l, acc = _flash_causal(qi, unroll, tk, scores, lambda kt: v_ref[kt], diff, s_scr,
                              _flash_init(MLA_V, tq))
    o = (acc / l).astype(o_ref.dtype)
    for s in range(nsub):
        o_ref[s] = o[:, s * tl:(s + 1) * tl]


def _mla_attn(qT, k, vT, B, S):
    tq, tk = MLA_TQ, MLA_TK
    tl = qT.shape[2]
    nsub = tq // tl
    nq = S // tq
    nkt = S // tk
    T = B * S
    return pl.pallas_call(
        functools.partial(_mla_attn_kernel, tq=tq, tk=tk),
        grid=(B, MLA_HEADS, nq),
        in_specs=[pl.BlockSpec((nsub, MLA_QK, tl), lambda b, h, i: (b * nq + i, h, 0)),
                  pl.BlockSpec((None, None, S, MLA_QK), lambda b, h, i: (b, h, 0, 0)),
                  pl.BlockSpec((nkt, MLA_V, tk), lambda b, h, i: (b, h, 0))],
        out_specs=pl.BlockSpec((nsub, MLA_V, tl), lambda b, h, i: (b * nq + i, h, 0)),
        out_shape=jax.ShapeDtypeStruct((T // tl, MLA_HEADS * MLA_V, tl), MXU_DTYPE),
        scratch_shapes=[pltpu.VMEM((tq // tk, tk, tq), F32)],
        compiler_params=_params("parallel", "parallel", "arbitrary"), name="mla_attn",
    )(qT, k, vT)


def _moe_router_kernel(x_ref, g_ref, w_ref, b_ref, id_ref, wt_ref, cnt_ref):
    tm = x_ref.shape[0]
    h = _rms_lanes(x_ref[...], g_ref[...]).astype(MXU_DTYPE)
    lg = _dot_nt(w_ref[...], h) + b_ref[...]
    gl = lg[:MOE_GROUPS]
    ge = jnp.exp(gl - jnp.max(gl, axis=0, keepdims=True))
    gp = ge / jnp.sum(ge, axis=0, keepdims=True)
    rid = lax.broadcasted_iota(jnp.int32, (MOE_GROUPS, 1), 0)
    g_w = jnp.max(gp, axis=0, keepdims=True)
    grp = jnp.min(jnp.where(gp == g_w, rid, MOE_GROUPS), axis=0, keepdims=True)
    el = jnp.zeros((MOE_EPG, tm), F32)
    for gi in range(MOE_GROUPS):
        rows = lg[MOE_GROUPS + gi * MOE_EPG:MOE_GROUPS + (gi + 1) * MOE_EPG]
        el = jnp.where(grp == gi, rows, el)
    ee = jnp.exp(el - jnp.max(el, axis=0, keepdims=True))
    ep = ee / jnp.sum(ee, axis=0, keepdims=True)
    p1 = jnp.max(ep, axis=0, keepdims=True)
    i1 = jnp.min(jnp.where(ep == p1, rid, MOE_EPG), axis=0, keepdims=True)
    ep2 = jnp.where(rid == i1, -1.0, ep)
    p2 = jnp.max(ep2, axis=0, keepdims=True)
    i2 = jnp.min(jnp.where(ep2 == p2, rid, MOE_EPG), axis=0, keepdims=True)
    den = p1 + p2
    zi = jnp.zeros((1, tm), jnp.int32)
    zf = jnp.zeros((1, tm), F32)
    e1 = grp * MOE_EPG + i1
    e2 = grp * MOE_EPG + i2
    id_ref[...] = jnp.concatenate([e1, e2] + [zi] * 6, axis=0)
    wt_ref[...] = jnp.concatenate([g_w * p1 / den, g_w * p2 / den] + [zf] * 6, axis=0)
    eid = lax.broadcasted_iota(jnp.int32, (MOE_EXPERTS, 1), 0)
    cnt_ref[...] = jnp.sum(((eid == e1) | (eid == e2)).astype(F32), axis=1, keepdims=True)


def _moe_router(x2, g, w_group, b_group, w_expert, b_expert):
    T, D = x2.shape
    tm = ROW_TILE
    nrow = 128
    pad = nrow - MOE_GROUPS - MOE_EXPERTS
    wT = jnp.concatenate([w_group, w_expert, jnp.zeros((D, pad), w_group.dtype)], axis=1).T.astype(MXU_DTYPE)
    bias = jnp.concatenate([b_group, b_expert, jnp.zeros((pad,), b_group.dtype)]).reshape(nrow, 1)
    return pl.pallas_call(
        _moe_router_kernel,
        grid=(T // tm,),
        in_specs=[pl.BlockSpec((tm, D), lambda i: (i, 0)), pl.BlockSpec((1, D), lambda i: (0, 0)),
                  pl.BlockSpec((nrow, D), lambda i: (0, 0)), pl.BlockSpec((nrow, 1), lambda i: (0, 0))],
        out_specs=(pl.BlockSpec((None, 8, tm), lambda i: (i, 0, 0)), pl.BlockSpec((None, 8, tm), lambda i: (i, 0, 0)),
                   pl.BlockSpec((None, MOE_EXPERTS, 1), lambda i: (i, 0, 0))),
        out_shape=(jax.ShapeDtypeStruct((T // tm, 8, tm), jnp.int32), jax.ShapeDtypeStruct((T // tm, 8, tm), F32),
                   jax.ShapeDtypeStruct((T // tm, MOE_EXPERTS, 1), F32)),
        compiler_params=_params("parallel"), name="moe_router",
    )(x2, g.reshape(1, D), wT, bias)


def _moe_plan_kernel(id_ref, base_ref, tri_ref, d_ref):
    tm = id_ref.shape[1]
    eid = lax.broadcasted_iota(jnp.int32, (MOE_EXPERTS, 1), 0)
    f0 = (eid == id_ref[0:1, :]).astype(F32)
    f1 = (eid == id_ref[1:2, :]).astype(F32)
    pre0 = _dot(f0.astype(MXU_DTYPE), tri_ref[...])
    pre1 = _dot(f1.astype(MXU_DTYPE), tri_ref[...])
    base = base_ref[...]
    tot0 = jnp.sum(f0, axis=1, keepdims=True)
    d0 = jnp.sum(f0 * (base + pre0), axis=0, keepdims=True)
    d1 = jnp.sum(f1 * (base + tot0 + pre1), axis=0, keepdims=True)
    zf = jnp.zeros((1, tm), F32)
    d_ref[...] = jnp.concatenate([d0, d1] + [zf] * 6, axis=0).astype(jnp.int32)


def _moe_plan(ids, cnt):
    nt, _, tm = ids.shape
    blk = MOE_BLOCK
    c = cnt.reshape(nt, MOE_EXPERTS)
    tot = jnp.sum(c, axis=0)
    padded = jnp.ceil(tot / blk) * blk
    pad_ends = jnp.cumsum(padded)
    tile_base = (pad_ends - padded)[None, :] + jnp.cumsum(c, axis=0) - c
    n_blocks = nt * tm * MOE_TOP_K // blk + MOE_EXPERTS
    block_e = jnp.minimum(jnp.searchsorted(pad_ends, jnp.arange(n_blocks, dtype=F32) * blk, side='right'),
                          MOE_EXPERTS - 1).astype(jnp.int32)
    n_used = (pad_ends[-1] / blk).astype(jnp.int32).reshape(1)
    tri = jnp.asarray(np.arange(tm)[:, None] < np.arange(tm)[None, :], MXU_DTYPE)
    dest = pl.pallas_call(
        _moe_plan_kernel,
        grid=(nt,),
        in_specs=[pl.BlockSpec((None, 8, tm), lambda i: (i, 0, 0)),
                  pl.BlockSpec((None, MOE_EXPERTS, 1), lambda i: (i, 0, 0)),
                  pl.BlockSpec((tm, tm), lambda i: (0, 0))],
        out_specs=pl.BlockSpec((None, 8, tm), lambda i: (i, 0, 0)),
        out_shape=jax.ShapeDtypeStruct((nt, 8, tm), jnp.int32),
        compiler_params=_params("parallel"), name="moe_plan",
    )(ids, tile_base.reshape(nt, MOE_EXPERTS, 1), tri)
    return dest, block_e, n_used, n_blocks


def _moe_scatter_kernel(d_ref, x_hbm, xs_zero, xs_hbm, sem):
    del xs_zero
    i = pl.program_id(0)
    n = d_ref.shape[2] // MOE_TOP_K

    def start(j, c):
        src = x_hbm.at[pl.ds(i * n + j, 1), :]
        for k in range(MOE_TOP_K):
            pltpu.make_async_copy(src, xs_hbm.at[pl.ds(d_ref[0, 0, k * n + j], 1), :], sem.at[0]).start()
        return c

    lax.fori_loop(0, n, start, 0, unroll=8)
    pltpu.make_async_copy(x_hbm.at[pl.ds(0, MOE_TOP_K * n), :], xs_hbm.at[pl.ds(0, MOE_TOP_K * n), :], sem.at[0]).wait()


def _moe_scatter(x2, dest, n_slots):
    T, D = x2.shape
    nt, _, tm = dest.shape
    d3 = dest[:, :MOE_TOP_K, :].reshape(nt, 1, MOE_TOP_K * tm)
    return pl.pallas_call(
        _moe_scatter_kernel,
        grid=(nt,),
        in_specs=[pl.BlockSpec((1, 1, MOE_TOP_K * tm), lambda i: (i, 0, 0), memory_space=pltpu.SMEM),
                  pl.BlockSpec(memory_space=pl.ANY), pl.BlockSpec(memory_space=pl.ANY)],
        out_specs=pl.BlockSpec(memory_space=pl.ANY),
        out_shape=jax.ShapeDtypeStruct((n_slots, D), F32),
        scratch_shapes=[pltpu.SemaphoreType.DMA((1,))],
        input_output_aliases={2: 0},
        compiler_params=_params("arbitrary"), name="moe_scatter",
    )(d3, x2, jnp.zeros((n_slots, D), F32))


def _moe_expert_kernel(be_ref, nu_ref, x_ref, g_ref, wg_ref, wu_ref, wd_ref, y_ref):
    i = pl.program_id(0)

    @pl.when(i < nu_ref[0])
    def _():
        h = _rms_lanes(x_ref[...], g_ref[...]).astype(MXU_DTYPE)
        a = _dot(h, wg_ref[...])
        u = _dot(h, wu_ref[...])
        mid = (jax.nn.silu(a) * u).astype(MXU_DTYPE)
        y_ref[...] = _dot(mid, wd_ref[...])

    @pl.when(i >= nu_ref[0])
    def _():
        y_ref[...] = jnp.zeros_like(y_ref)


def _moe_experts(xs, g, block_e, n_used, w_gate, w_up, w_down):
    n_slots, D = xs.shape
    blk = MOE_BLOCK
    n_blocks = n_slots // blk
    grid_spec = pltpu.PrefetchScalarGridSpec(
        num_scalar_prefetch=2,
        grid=(n_blocks,),
        in_specs=[
            pl.BlockSpec((blk, D), lambda i, be, nu: (i, 0)),
            pl.BlockSpec((1, D), lambda i, be, nu: (0, 0)),
            pl.BlockSpec((None, D, MOE_FF), lambda i, be, nu: (be[i], 0, 0)),
            pl.BlockSpec((None, D, MOE_FF), lambda i, be, nu: (be[i], 0, 0)),
            pl.BlockSpec((None, MOE_FF, D), lambda i, be, nu: (be[i], 0, 0)),
        ],
        out_specs=pl.BlockSpec((blk, D), lambda i, be, nu: (i, 0)),
    )
    return pl.pallas_call(
        _moe_expert_kernel, grid_spec=grid_spec,
        out_shape=jax.ShapeDtypeStruct((n_slots, D), F32),
        compiler_params=_params("arbitrary"), name="moe_experts",
    )(block_e, n_used, xs, g.reshape(1, D), w_gate.astype(MXU_DTYPE), w_up.astype(MXU_DTYPE), w_down.astype(MXU_DTYPE))


def _moe_combine_kernel(d_ref, y_hbm, x_ref, w_ref, o_ref, ybuf, sem):
    tc = x_ref.shape[0]

    def start(r, c):
        pltpu.make_async_copy(y_hbm.at[pl.ds(d_ref[0, 0, r], 1), :], ybuf.at[pl.ds(r, 1), :], sem.at[0]).start()
        return c

    lax.fori_loop(0, MOE_TOP_K * tc, start, 0, unroll=8)
    pltpu.make_async_copy(y_hbm.at[pl.ds(0, MOE_TOP_K * tc), :], ybuf, sem.at[0]).wait()
    w = w_ref[...]
    o_ref[...] = x_ref[...] + w[:, 0:1] * ybuf[:tc, :] + w[:, 1:2] * ybuf[tc:, :]


def _moe_combine(x2, y, dest, wts):
    T, D = x2.shape
    nt, _, tm = dest.shape
    tc = MOE_TC
    d3 = dest[:, :MOE_TOP_K, :].reshape(nt, MOE_TOP_K, tm // tc, tc).transpose(0, 2, 1, 3).reshape(T // tc, 1, MOE_TOP_K * tc)
    w2 = wts[:, :MOE_TOP_K, :].transpose(0, 2, 1).reshape(T, MOE_TOP_K)
    return pl.pallas_call(
        _moe_combine_kernel,
        grid=(T // tc,),
        in_specs=[pl.BlockSpec((1, 1, MOE_TOP_K * tc), lambda i: (i, 0, 0), memory_space=pltpu.SMEM),
                  pl.BlockSpec(memory_space=pl.ANY),
                  pl.BlockSpec((tc, D), lambda i: (i, 0)),
                  pl.BlockSpec((tc, MOE_TOP_K), lambda i: (i, 0))],
        out_specs=pl.BlockSpec((tc, D), lambda i: (i, 0)),
        out_shape=jax.ShapeDtypeStruct((T, D), F32),
        scratch_shapes=[pltpu.VMEM((MOE_TOP_K * tc, D), F32), pltpu.SemaphoreType.DMA((1,))],
        compiler_params=_params("arbitrary"), name="moe_combine",
    )(d3, y, x2, w2)


def _hier_moe(x2, g, w_group, b_group, w_expert, b_expert, w_gate, w_up, w_down):
    ids, wts, cnt = _moe_router(x2, g, w_group, b_group, w_expert, b_expert)
    dest, block_e, n_used, n_blocks = _moe_plan(ids, cnt)
    xs = _moe_scatter(x2, dest, n_blocks * MOE_BLOCK)
    y = _moe_experts(xs, g, block_e, n_used, w_gate, w_up, w_down)
    return _moe_combine(x2, y, dest, wts)


def _rope_tables_T(seq, dim):
    inv = 1.0 / (ROPE_THETA ** (jnp.arange(0, dim, 2, dtype=F32) / dim))
    ang = jnp.arange(seq, dtype=F32)[:, None] * inv[None, :]
    return jnp.cos(ang).T, jnp.sin(ang).T


def _nsa_layer(x2, g, w_in, cmp_pos, cmp_w1, cmp_w2, q_norm, k_norm, w_out, cosT, sinT, B, S):
    qn, qr, ks, kw, vsT, vwT, gT, kc, vc = _nsa_proj(x2, g, w_in, q_norm, k_norm, cosT, sinT, B, S)
    kcmp = _compress(kc, cmp_pos[0], cmp_w1[0], cmp_w2[0], k_norm[0], B, S, True)
    vcmpT = _compress(vc, cmp_pos[1], cmp_w1[1], cmp_w2[1], k_norm[0], B, S, False)
    T = B * S
    gates5 = gT.reshape(T // NSA_TQ, 3, NSA_GROUPS, NSA_REP, NSA_TQ)
    ocg, selb = _nsa_cmp(qn, kcmp, vcmpT, gates5, B, S)
    aT = _nsa_attn(qr, ks, vsT, kw, vwT, selb, ocg, gates5, B, S)
    return _out_proj(aT, x2, w_out)


def _mla_layer(x2, g, w_in, q_lat_norm, kv_lat_norm, w_uq, w_ukv, qn_norm, qr_norm, kn_norm, kr_norm,
               w_out, cosT, sinT, B, S):
    qT, k, vT = _mla_proj(x2, g, w_in, q_lat_norm, kv_lat_norm, w_uq, w_ukv, qn_norm, qr_norm, kn_norm, kr_norm,
                          cosT, sinT, B, S)
    aT = _mla_attn(qT, k, vT, B, S)
    return _out_proj(aT, x2, w_out)


def kernel(x, norm_mix, norm_ffn, nsa_w_in, nsa_cmp_pos, nsa_cmp_w1, nsa_cmp_w2, nsa_q_norm, nsa_k_norm, nsa_w_out, mla_w_in, mla_q_lat_norm, mla_kv_lat_norm, mla_w_uq, mla_w_ukv, mla_q_nope_norm, mla_q_rope_norm, mla_k_nope_norm, mla_k_rope_norm, mla_w_out, moe_w_group, moe_b_group, moe_w_expert, moe_b_expert, moe_w_gate, moe_w_up, moe_w_down):
    B, S, D = x.shape
    depth = norm_mix.shape[0]
    assert S % max(ROW_TILE, NSA_TQ, MLA_TQ) == 0 and MLA_TQ % ROW_TILE == 0 and S >= WINDOW
    cosT, sinT = _rope_tables_T(S, NSA_HEAD_DIM)
    x2 = x.reshape(B * S, D)
    for i in range(depth):
        j = i // 2
        if i % 2 == 0:
            x2 = _nsa_layer(x2, norm_mix[i], nsa_w_in[j], nsa_cmp_pos[j], nsa_cmp_w1[j], nsa_cmp_w2[j],
                            nsa_q_norm[j], nsa_k_norm[j], nsa_w_out[j], cosT, sinT, B, S)
        else:
            x2 = _mla_layer(x2, norm_mix[i], mla_w_in[j], mla_q_lat_norm[j], mla_kv_lat_norm[j], mla_w_uq[j],
                            mla_w_ukv[j], mla_q_nope_norm[j], mla_q_rope_norm[j], mla_k_nope_norm[j],
                            mla_k_rope_norm[j], mla_w_out[j], cosT, sinT, B, S)
        x2 = _hier_moe(x2, norm_ffn[i], moe_w_group[i], moe_b_group[i], moe_w_expert[i], moe_b_expert[i],
                       moe_w_gate[i], moe_w_up[i], moe_w_down[i])
    return x2.reshape(B, S, D)
```

```python
import functools

import numpy as np
import jax
import jax.numpy as jnp
from jax import lax
from jax.experimental import pallas as pl
from jax.experimental.pallas import tpu as pltpu

RMS_EPS = 1e-6
ROPE_THETA = 10000.0
NEG_INF = -1e30

NSA_HEADS = 16
NSA_HEAD_DIM = 64
NSA_GROUPS = 4
NSA_REP = NSA_HEADS // NSA_GROUPS
CMP_LEN = 32
CMP_STRIDE = 16
CMP_HIDDEN = 128
SEL_LEN = 64
SEL_RATIO = SEL_LEN // CMP_STRIDE
SEL_TOPK = 16
SEL_FORCE = 1e4
WINDOW = 512
NSA_Q_DIM = NSA_HEADS * NSA_HEAD_DIM
NSA_KV_DIM = NSA_GROUPS * NSA_HEAD_DIM

MLA_HEADS = 8
MLA_NOPE = 128
MLA_ROPE = 64
MLA_V = 128
MLA_QK = MLA_NOPE + MLA_ROPE
MLA_Q_LORA = 256
MLA_KV_LORA = 256

MOE_GROUPS = 8
MOE_EPG = 8
MOE_EXPERTS = MOE_GROUPS * MOE_EPG
MOE_TOP_K = 2
MOE_FF = 256

MXU_DTYPE = jnp.bfloat16
ROW_TILE = 512
NSA_TQ = 512
NSA_TK = 256
MLA_TQ = 1024
MLA_TK = 256
MOE_BLOCK = 256
MOE_TC = 256
VMEM_LIMIT = 56 * 1024 * 1024

_NT = (((1,), (1,)), ((), ()))
F32 = jnp.float32
LOG2E = 1.4426950408889634


def _params(*sem):
    return pltpu.CompilerParams(dimension_semantics=sem, vmem_limit_bytes=VMEM_LIMIT)


def _dot(a, b):
    return jnp.dot(a, b, preferred_element_type=F32)


def _dot_nt(a, b):
    return lax.dot_general(a, b, _NT, preferred_element_type=F32)


def _rms_lanes(x, g):
    ms = jnp.mean(x * x, axis=-1, keepdims=True)
    return x * lax.rsqrt(ms + RMS_EPS) * g


def _rms_rows(y, g):
    ms = jnp.mean(y * y, axis=-2, keepdims=True)
    return y * lax.rsqrt(ms + RMS_EPS) * g


def _rope_rows(y, cos, sin):
    half = y.shape[-2] // 2
    y1 = y[..., :half, :]
    y2 = y[..., half:, :]
    return jnp.concatenate([y1 * cos - y2 * sin, y2 * cos + y1 * sin], axis=-2)


def _nsa_proj_kernel(x_ref, g_ref, wq_ref, wk_ref, wv_ref, wg_ref, wc_ref, cos_ref, sin_ref,
                     qg_ref, kg_ref, blk_ref,
                     qn_ref, qr_ref, ks_ref, kw_ref, vs_ref, vw_ref, gt_ref, kc_ref, vc_ref,
                     *, tq, tk):
    tm = x_ref.shape[0]
    h = _rms_lanes(x_ref[...], g_ref[...]).astype(MXU_DTYPE)
    cos = cos_ref[...]
    sin = sin_ref[...]
    scale = NSA_HEAD_DIM ** -0.5 * LOG2E

    q3 = _dot_nt(wq_ref[...], h).reshape(NSA_HEADS, NSA_HEAD_DIM, tm)
    qn = _rms_rows(q3, qg_ref[...])
    qr = _rope_rows(qn, cos, sin)
    qn = (qn * scale).astype(MXU_DTYPE).reshape(NSA_Q_DIM, tm)
    qr = (qr * scale).astype(MXU_DTYPE).reshape(NSA_Q_DIM, tm)
    for s in range(tm // tq):
        qn_ref[s] = qn[:, s * tq:(s + 1) * tq]
        qr_ref[s] = qr[:, s * tq:(s + 1) * tq]

    k4 = _dot_nt(wk_ref[...], h).reshape(2, NSA_GROUPS, NSA_HEAD_DIM, tm)
    for br, o_ref in ((0, ks_ref), (1, kw_ref)):
        kb = _rope_rows(_rms_rows(k4[br], kg_ref[br]), cos, sin)
        kb = kb.reshape(NSA_KV_DIM, tm).T.astype(MXU_DTYPE)
        for g in range(NSA_GROUPS):
            kg = kb[:, g * NSA_HEAD_DIM:(g + 1) * NSA_HEAD_DIM]
            o_ref[g] = jnp.concatenate([kg, blk_ref[...]], axis=1) if br == 0 else kg

    vT = _dot_nt(wv_ref[...], h).astype(MXU_DTYPE)
    for s in range(tm // tk):
        vs_ref[s] = vT[:NSA_KV_DIM, s * tk:(s + 1) * tk]
        vw_ref[s] = vT[NSA_KV_DIM:, s * tk:(s + 1) * tk]

    gT = jax.nn.sigmoid(_dot_nt(wg_ref[...], h))
    for s in range(tm // tq):
        gt_ref[s] = gT[:, s * tq:(s + 1) * tq]

    c = _dot(h, wc_ref[...])
    kc_ref[...] = c[:, :NSA_KV_DIM]
    vc_ref[...] = c[:, NSA_KV_DIM:]


def _nsa_proj(x2, g, w_in, q_norm, k_norm, cosT, sinT, B, S):
    T, D = x2.shape
    tm, tq, tk = ROW_TILE, NSA_TQ, NSA_TK
    nps = S // tm
    o = NSA_Q_DIM
    kvd = NSA_KV_DIM
    wq = w_in[:, :o].T.astype(MXU_DTYPE)
    wc = w_in[:, o:o + 2 * kvd].astype(MXU_DTYPE)
    wk = jnp.concatenate([w_in[:, o + 2 * kvd:o + 3 * kvd], w_in[:, o + 4 * kvd:o + 5 * kvd]], axis=1).T.astype(MXU_DTYPE)
    wv = jnp.concatenate([w_in[:, o + 3 * kvd:o + 4 * kvd], w_in[:, o + 5 * kvd:o + 6 * kvd]], axis=1).T.astype(MXU_DTYPE)
    wg = w_in[:, o + 6 * kvd:].T.astype(MXU_DTYPE)
    ng = wg.shape[0]
    nsel = S // SEL_LEN
    assert NSA_HEAD_DIM + nsel <= 256
    blk_onehot = jnp.asarray(np.arange(S)[:, None] // SEL_LEN == np.arange(nsel)[None, :], MXU_DTYPE)
    full = lambda shape: pl.BlockSpec(shape, lambda i: (0,) * len(shape))
    out_shapes = (
        jax.ShapeDtypeStruct((T // tq, o, tq), MXU_DTYPE),
        jax.ShapeDtypeStruct((T // tq, o, tq), MXU_DTYPE),
        jax.ShapeDtypeStruct((B, NSA_GROUPS, S, NSA_HEAD_DIM + nsel), MXU_DTYPE),
        jax.ShapeDtypeStruct((B, NSA_GROUPS, S, NSA_HEAD_DIM), MXU_DTYPE),
        jax.ShapeDtypeStruct((T // tk, kvd, tk), MXU_DTYPE),
        jax.ShapeDtypeStruct((T // tk, kvd, tk), MXU_DTYPE),
        jax.ShapeDtypeStruct((T // tq, ng, tq), F32),
        jax.ShapeDtypeStruct((T, kvd), F32),
        jax.ShapeDtypeStruct((T, kvd), F32),
    )
    kspec = pl.BlockSpec((None, NSA_GROUPS, tm, NSA_HEAD_DIM), lambda i: (i // nps, 0, i % nps, 0))
    kaspec = pl.BlockSpec((None, NSA_GROUPS, tm, NSA_HEAD_DIM + nsel), lambda i: (i // nps, 0, i % nps, 0))
    out_specs = (
        pl.BlockSpec((tm // tq, o, tq), lambda i: (i, 0, 0)),
        pl.BlockSpec((tm // tq, o, tq), lambda i: (i, 0, 0)),
        kaspec, kspec,
        pl.BlockSpec((tm // tk, kvd, tk), lambda i: (i, 0, 0)),
        pl.BlockSpec((tm // tk, kvd, tk), lambda i: (i, 0, 0)),
        pl.BlockSpec((tm // tq, ng, tq), lambda i: (i, 0, 0)),
        pl.BlockSpec((tm, kvd), lambda i: (i, 0)),
        pl.BlockSpec((tm, kvd), lambda i: (i, 0)),
    )
    in_specs = [
        pl.BlockSpec((tm, D), lambda i: (i, 0)),
        full((1, D)), full(wq.shape), full(wk.shape), full(wv.shape), full(wg.shape), full(wc.shape),
        pl.BlockSpec((NSA_HEAD_DIM // 2, tm), lambda i: (0, i % nps)),
        pl.BlockSpec((NSA_HEAD_DIM // 2, tm), lambda i: (0, i % nps)),
        full((NSA_HEAD_DIM, 1)), full((2, NSA_HEAD_DIM, 1)),
        pl.BlockSpec((tm, nsel), lambda i: (i % nps, 0)),
    ]
    return pl.pallas_call(
        functools.partial(_nsa_proj_kernel, tq=tq, tk=tk),
        grid=(T // tm,), in_specs=in_specs, out_specs=out_specs, out_shape=out_shapes,
        compiler_params=_params("parallel"), name="nsa_proj",
    )(x2, g.reshape(1, D), wq, wk, wv, wg, wc, cosT, sinT,
      q_norm.reshape(NSA_HEAD_DIM, 1), k_norm[1:3].reshape(2, NSA_HEAD_DIM, 1), blk_onehot)


def _compress_kernel(x_ref, pa_ref, pb_ref, w1a_ref, w1b_ref, w2t_ref, gain_ref, o_ref, *, normalize, kchunk):
    n, kdim = x_ref.shape
    hdim = w1a_ref.shape[1]
    h1 = jnp.zeros((n, hdim), F32)
    h2 = jnp.zeros((n, hdim), F32)
    for c in range(kdim // kchunk):
        sl = slice(c * kchunk, (c + 1) * kchunk)
        xs = x_ref[:, sl]
        h1 = h1 + _dot((xs + pa_ref[:, sl]).astype(MXU_DTYPE), w1a_ref[sl, :])
        h2 = h2 + _dot((xs + pb_ref[:, sl]).astype(MXU_DTYPE), w1b_ref[sl, :])
    hid = h1 + pltpu.roll(h2, n - 1, axis=0)
    act = jax.nn.gelu(hid).astype(MXU_DTYPE)
    oT = _dot_nt(w2t_ref[...], act)
    if normalize:
        o3 = _rms_rows(oT.reshape(NSA_GROUPS, NSA_HEAD_DIM, n), gain_ref[...])
        o = o3.reshape(NSA_KV_DIM, n).T.astype(MXU_DTYPE)
        for g in range(NSA_GROUPS):
            o_ref[g] = o[:, g * NSA_HEAD_DIM:(g + 1) * NSA_HEAD_DIM]
    else:
        o_ref[...] = oT.astype(MXU_DTYPE)


def _compress_weights(pos, w1, w2):
    G, Dh, H = NSA_GROUPS, NSA_HEAD_DIM, CMP_HIDDEN
    eye = jnp.eye(G, dtype=w1.dtype)

    def expand(w1_half):
        w = w1_half.reshape(CMP_STRIDE, Dh, H)
        return jnp.einsum('rdj,gk->rgdkj', w, eye).reshape(CMP_STRIDE * G * Dh, G * H).astype(MXU_DTYPE)

    half = CMP_STRIDE * Dh
    w1a, w1b = expand(w1[:half]), expand(w1[half:])
    tilepos = lambda p: jnp.broadcast_to(p[:, None, :], (CMP_STRIDE, G, Dh)).reshape(1, CMP_STRIDE * G * Dh)
    pa, pb = tilepos(pos[:CMP_STRIDE]), tilepos(pos[CMP_STRIDE:])
    w2t = jnp.einsum('jd,gk->kdgj', w2, eye).reshape(G * Dh, G * H).astype(MXU_DTYPE)
    return pa, pb, w1a, w1b, w2t


def _compress(xc, pos, w1, w2, gain, B, S, normalize):
    n = S // CMP_STRIDE
    kdim = CMP_STRIDE * NSA_KV_DIM
    xr = xc.reshape(B * n, kdim)
    pa, pb, w1a, w1b, w2t = _compress_weights(pos, w1, w2)
    full = lambda shape: pl.BlockSpec(shape, lambda b: (0,) * len(shape))
    if normalize:
        out_shape = jax.ShapeDtypeStruct((B, NSA_GROUPS, n, NSA_HEAD_DIM), MXU_DTYPE)
        out_spec = pl.BlockSpec((None, NSA_GROUPS, n, NSA_HEAD_DIM), lambda b: (b, 0, 0, 0))
    else:
        out_shape = jax.ShapeDtypeStruct((B, NSA_KV_DIM, n), MXU_DTYPE)
        out_spec = pl.BlockSpec((None, NSA_KV_DIM, n), lambda b: (b, 0, 0))
    return pl.pallas_call(
        functools.partial(_compress_kernel, normalize=normalize, kchunk=1024),
        grid=(B,),
        in_specs=[pl.BlockSpec((n, kdim), lambda b: (b, 0)), full(pa.shape), full(pb.shape),
                  full(w1a.shape), full(w1b.shape), full(w2t.shape), full((NSA_HEAD_DIM, 1))],
        out_specs=out_spec, out_shape=out_shape,
        compiler_params=_params("parallel"), name="nsa_compress",
    )(xr, pa, pb, w1a, w1b, w2t, gain.reshape(NSA_HEAD_DIM, 1))


def _stack_heads(ref):
    return jnp.concatenate([ref[r * NSA_HEAD_DIM:(r + 1) * NSA_HEAD_DIM, :] for r in range(NSA_REP)], axis=1)


def _nsa_cmp_kernel(qn_ref, kc_ref, vct_ref, wimp_ref, gt_ref, oc_ref, sel_ref, *, tq, sel_k):
    qi = pl.program_id(2)
    q0 = qi * tq
    ncmp = kc_ref.shape[0]
    nsel = sel_ref.shape[0]
    lanes = NSA_REP * tq
    qT = _stack_heads(qn_ref)
    s = _dot(kc_ref[...], qT)
    t4 = q0 + lax.broadcasted_iota(jnp.int32, (1, lanes), 1) % tq
    cmp_end = lax.broadcasted_iota(jnp.int32, (ncmp, 1), 0) * CMP_STRIDE + (CMP_LEN - 1)
    mask = cmp_end <= t4
    s = jnp.where(mask, s, NEG_INF)
    m = jnp.max(s, axis=0, keepdims=True)
    e = jnp.where(mask, jnp.exp2(s - m), 0.0)
    p = e / jnp.maximum(jnp.sum(e, axis=0, keepdims=True), 1e-30)
    ocT = _dot(vct_ref[...], p.astype(MXU_DTYPE))
    for r in range(NSA_REP):
        oc_ref[r * NSA_HEAD_DIM:(r + 1) * NSA_HEAD_DIM, :] = ocT[:, r * tq:(r + 1) * tq] * gt_ref[0, r:r + 1, :]

    psum = p[:, :tq]
    for r in range(1, NSA_REP):
        psum = psum + p[:, r * tq:(r + 1) * tq]
    hi = psum.astype(MXU_DTYPE)
    lo = (psum - hi.astype(F32)).astype(MXU_DTYPE)
    imp = _dot(wimp_ref[...], hi) + _dot(wimp_ref[...], lo)

    t = q0 + lax.broadcasted_iota(jnp.int32, (1, tq), 1)
    j = lax.broadcasted_iota(jnp.int32, (nsel, 1), 0)
    cur = t // SEL_LEN
    valid = j * SEL_LEN <= t
    forced = (j == 0) | (j == cur) | (j == cur - 1)
    score = jnp.where(forced, SEL_FORCE, jnp.where(valid, imp, -1.0))
    jf = j.astype(F32)
    sel = jnp.zeros((nsel, tq), jnp.bool_)
    for _ in range(sel_k):
        mx = jnp.max(score, axis=0, keepdims=True)
        first = jnp.min(jnp.where(score == mx, jf, float(nsel)), axis=0, keepdims=True)
        hit = jf == first
        sel = sel | hit
        score = jnp.where(hit, -jnp.inf, score)
    sel_ref[...] = jnp.where(sel, 0.0, NEG_INF).astype(sel_ref.dtype)


def _imp_weights(S):
    nsel, ncmp_pad = S // SEL_LEN, S // CMP_STRIDE
    ncmp = ncmp_pad - (CMP_LEN // CMP_STRIDE) + 1
    w = np.zeros((nsel, ncmp_pad), np.float32)
    for jj in range(nsel):
        for off in range(-(CMP_LEN // CMP_STRIDE - 1), SEL_RATIO):
            i = SEL_RATIO * jj + off
            lo = max(off * CMP_STRIDE, 0)
            hi = min(off * CMP_STRIDE + CMP_LEN - 1, SEL_LEN - 1)
            if 0 <= i < ncmp:
                w[jj, i] = (hi - lo + 1) / CMP_STRIDE
    return jnp.asarray(w, MXU_DTYPE)


def _nsa_cmp(qn, kcmp, vcmpT, gates5, B, S):
    tq = NSA_TQ
    nq = S // tq
    ncmp = S // CMP_STRIDE
    nsel = S // SEL_LEN
    wimp = _imp_weights(S)
    T = B * S
    qdim = NSA_REP * NSA_HEAD_DIM
    return pl.pallas_call(
        functools.partial(_nsa_cmp_kernel, tq=tq, sel_k=min(SEL_TOPK, nsel)),
        grid=(B, NSA_GROUPS, nq),
        in_specs=[
            pl.BlockSpec((None, qdim, tq), lambda b, g, i: (b * nq + i, g, 0)),
            pl.BlockSpec((None, None, ncmp, NSA_HEAD_DIM), lambda b, g, i: (b, g, 0, 0)),
            pl.BlockSpec((None, NSA_HEAD_DIM, ncmp), lambda b, g, i: (b, g, 0)),
            pl.BlockSpec((nsel, ncmp), lambda b, g, i: (0, 0)),
            pl.BlockSpec((None, 3, None, NSA_REP, tq), lambda b, g, i: (b * nq + i, 0, g, 0, 0)),
        ],
        out_specs=(
            pl.BlockSpec((None, qdim, tq), lambda b, g, i: (b * nq + i, g, 0)),
            pl.BlockSpec((None, None, None, nsel, tq), lambda b, g, i: (b, g, i, 0, 0)),
        ),
        out_shape=(
            jax.ShapeDtypeStruct((T // tq, NSA_Q_DIM, tq), F32),
            jax.ShapeDtypeStruct((B, NSA_GROUPS, nq, nsel, tq), MXU_DTYPE),
        ),
        compiler_params=_params("parallel", "parallel", "parallel"), name="nsa_cmp_select",
    )(qn, kcmp, vcmpT, wimp, gates5)


def _flash_step(carry, s, vT):
    m, l, acc = carry
    m_new = jnp.maximum(m, jnp.max(s, axis=0, keepdims=True))
    alpha = jnp.exp2(m - m_new)
    p = jnp.exp2(s - m_new)
    l = alpha * l + jnp.sum(p, axis=0, keepdims=True)
    acc = alpha * acc + _dot(vT, p.astype(MXU_DTYPE))
    return m_new, l, acc


def _flash_init(dv, nq):
    return (jnp.full((1, nq), NEG_INF, F32), jnp.zeros((1, nq), F32), jnp.zeros((dv, nq), F32))


def _flash_causal(qi, unroll, tk, scores, values, diff, s_scr, carry):
    for u in range(unroll):
        s_scr[u] = scores(u)

    def body(i, carry):
        for u in range(unroll):
            m, l, acc = carry
            s = s_scr[u]
            m_new = jnp.maximum(m, jnp.max(s, axis=0, keepdims=True))
            alpha = jnp.exp2(m - m_new)
            p = jnp.exp2(s - m_new)
            l = alpha * l + jnp.sum(p, axis=0, keepdims=True)
            s_scr[u] = scores((i + 1) * unroll + u)
            acc = alpha * acc + _dot(values(i * unroll + u), p.astype(MXU_DTYPE))
            carry = (m_new, l, acc)
        return carry

    carry = lax.fori_loop(0, qi, body, carry)
    for d in range(unroll):
        s = jnp.where(diff <= -d * tk, s_scr[d], NEG_INF)
        carry = _flash_step(carry, s, values(qi * unroll + d))
    return carry


def _nsa_attn_kernel(qr_ref, ks_ref, vs_ref, kw_ref, vw_ref, sel_ref, oc_ref, gt_ref, o_ref, s_scr, *, tq, tk):
    qi = pl.program_id(2)
    lanes = NSA_REP * tq
    dh = NSA_HEAD_DIM
    unroll = tq // tk
    qT = _stack_heads(qr_ref)
    q_aug = jnp.concatenate([qT, jnp.concatenate([sel_ref[...]] * NSA_REP, axis=1)], axis=0)
    tl = lax.broadcasted_iota(jnp.int32, (1, lanes), 1) % tq
    diff = lax.broadcasted_iota(jnp.int32, (tk, 1), 0) - tl

    def win_scores(d):
        kt = qi * unroll + d
        mask = diff <= -d * tk if d >= 0 else None
        lo_thr = -d * tk - WINDOW
        if d < 0:
            lo = diff > jnp.where(kt >= 0, lo_thr, 2 ** 30)
            kt = jnp.maximum(kt, 0)
        else:
            lo = diff > lo_thr if lo_thr >= -(tq - 1) else None
        if lo is not None:
            mask = lo if mask is None else mask & lo
        s = _dot(kw_ref[pl.ds(pl.multiple_of(kt * tk, tk), tk), :], qT)
        return s if mask is None else jnp.where(mask, s, NEG_INF)

    order = [0] + list(range(1, unroll)) + list(range(-(WINDOW // tk), 0))
    carry = _flash_init(dh, lanes)
    nxt = win_scores(order[0])
    for i, d in enumerate(order):
        m, l, acc = carry
        s = nxt
        m_new = jnp.maximum(m, jnp.max(s, axis=0, keepdims=True))
        alpha = jnp.exp2(m - m_new)
        p = jnp.exp2(s - m_new)
        l = alpha * l + jnp.sum(p, axis=0, keepdims=True)
        if i + 1 < len(order):
            nxt = win_scores(order[i + 1])
        acc = alpha * acc + _dot(vw_ref[jnp.maximum(qi * unroll + d, 0)], p.astype(MXU_DTYPE))
        carry = (m_new, l, acc)
    _, l_w, acc_w = carry

    def sel_scores(kt):
        return _dot(ks_ref[pl.ds(pl.multiple_of(kt * tk, tk), tk), :], q_aug)

    _, l_s, acc_s = _flash_causal(qi, unroll, tk, sel_scores, lambda kt: vs_ref[kt], diff, s_scr,
                                  _flash_init(dh, lanes))

    o_s = acc_s / l_s
    o_w = acc_w / l_w
    for r in range(NSA_REP):
        ls = slice(r * tq, (r + 1) * tq)
        rs = slice(r * dh, (r + 1) * dh)
        o = oc_ref[rs, :] + gt_ref[1, r:r + 1, :] * o_s[:, ls] + gt_ref[2, r:r + 1, :] * o_w[:, ls]
        o_ref[rs, :] = o.astype(o_ref.dtype)


def _nsa_attn(qr, ks, vsT, kw, vwT, selb, ocg, gates5, B, S):
    tq, tk = NSA_TQ, NSA_TK
    nq = S // tq
    nkt = S // tk
    nsel = S // SEL_LEN
    T = B * S
    qdim = NSA_REP * NSA_HEAD_DIM
    assert tq % tk == 0 and WINDOW % tk == 0 and tq <= WINDOW
    kspec = pl.BlockSpec((None, None, S, NSA_HEAD_DIM), lambda b, g, i: (b, g, 0, 0))
    kaspec = pl.BlockSpec((None, None, S, NSA_HEAD_DIM + nsel), lambda b, g, i: (b, g, 0, 0))
    vspec = pl.BlockSpec((nkt, NSA_HEAD_DIM, tk), lambda b, g, i: (b, g, 0))
    qspec = pl.BlockSpec((None, qdim, tq), lambda b, g, i: (b * nq + i, g, 0))
    return pl.pallas_call(
        functools.partial(_nsa_attn_kernel, tq=tq, tk=tk),
        grid=(B, NSA_GROUPS, nq),
        in_specs=[
            qspec, kaspec, vspec, kspec, vspec,
            pl.BlockSpec((None, None, None, nsel, tq), lambda b, g, i: (b, g, i, 0, 0)),
            qspec,
            pl.BlockSpec((None, 3, None, NSA_REP, tq), lambda b, g, i: (b * nq + i, 0, g, 0, 0)),
        ],
        out_specs=qspec,
        out_shape=jax.ShapeDtypeStruct((T // tq, NSA_Q_DIM, tq), MXU_DTYPE),
        scratch_shapes=[pltpu.VMEM((tq // tk, tk, NSA_REP * tq), F32)],
        compiler_params=_params("parallel", "parallel", "arbitrary"), name="nsa_attn",
    )(qr, ks, vsT, kw, vwT, selb, ocg, gates5)


def _out_proj_kernel(a_ref, x_ref, w_ref, o_ref):
    nsub, _, tq = a_ref.shape
    for s in range(nsub):
        yT = _dot(w_ref[...], a_ref[s])
        o_ref[s * tq:(s + 1) * tq, :] = x_ref[s * tq:(s + 1) * tq, :] + yT.T


def _out_proj(aT, x2, w_out):
    T, D = x2.shape
    nt, feat, tq = aT.shape
    tm = ROW_TILE
    wT = w_out.T.astype(MXU_DTYPE)
    return pl.pallas_call(
        _out_proj_kernel,
        grid=(T // tm,),
        in_specs=[pl.BlockSpec((tm // tq, feat, tq), lambda i: (i, 0, 0)),
                  pl.BlockSpec((tm, D), lambda i: (i, 0)),
                  pl.BlockSpec((D, feat), lambda i: (0, 0))],
        out_specs=pl.BlockSpec((tm, D), lambda i: (i, 0)),
        out_shape=jax.ShapeDtypeStruct((T, D), F32),
        compiler_params=_params("parallel"), name="out_proj",
    )(aT, x2, wT)


def _mla_proj_kernel(x_ref, g_ref, win_ref, wuq_ref, wukv_ref, cos_ref, sin_ref,
                     gq_ref, gkv_ref, gqn_ref, gqr_ref, gkn_ref, gkr_ref,
                     q_ref, k_ref, v_ref, *, tq, tk):
    tm = x_ref.shape[0]
    h = _rms_lanes(x_ref[...], g_ref[...]).astype(MXU_DTYPE)
    cos = cos_ref[...]
    sin = sin_ref[...]
    cT = _dot_nt(win_ref[...], h)
    cq = _rms_rows(cT[:MLA_Q_LORA], gq_ref[...]).astype(MXU_DTYPE)
    ckv = _rms_rows(cT[MLA_Q_LORA:MLA_Q_LORA + MLA_KV_LORA], gkv_ref[...]).astype(MXU_DTYPE)
    kr = _rope_rows(_rms_rows(cT[MLA_Q_LORA + MLA_KV_LORA:], gkr_ref[...]), cos, sin)

    scale = MLA_QK ** -0.5 * LOG2E
    q3 = _dot(wuq_ref[...], cq).reshape(MLA_HEADS, MLA_QK, tm)
    qn = _rms_rows(q3[:, :MLA_NOPE, :], gqn_ref[...])
    qr = _rope_rows(_rms_rows(q3[:, MLA_NOPE:, :], gqr_ref[...]), cos, sin)
    q = (jnp.concatenate([qn, qr], axis=1) * scale).astype(MXU_DTYPE).reshape(MLA_HEADS * MLA_QK, tm)
    for s in range(tm // tq):
        q_ref[s] = q[:, s * tq:(s + 1) * tq]

    kv3 = _dot(wukv_ref[...], ckv).reshape(MLA_HEADS, MLA_NOPE + MLA_V, tm)
    vT = kv3[:, MLA_NOPE:, :].astype(MXU_DTYPE).reshape(MLA_HEADS * MLA_V, tm)
    for s in range(tm // tk):
        v_ref[s] = vT[:, s * tk:(s + 1) * tk]
    kn = _rms_rows(kv3[:, :MLA_NOPE, :], gkn_ref[...])
    kr_t = kr.T.astype(MXU_DTYPE)
    for hh in range(MLA_HEADS):
        k_ref[hh] = jnp.concatenate([kn[hh].T.astype(MXU_DTYPE), kr_t], axis=1)


def _mla_proj(x2, g, w_in, q_lat_norm, kv_lat_norm, w_uq, w_ukv, qn_norm, qr_norm, kn_norm, kr_norm,
              cosT, sinT, B, S):
    T, D = x2.shape
    tm, tq, tk = ROW_TILE, ROW_TILE, MLA_TK
    nps = S // tm
    winT = w_in.T.astype(MXU_DTYPE)
    wuqT = w_uq.T.astype(MXU_DTYPE)
    wukvT = w_ukv.T.astype(MXU_DTYPE)
    full = lambda shape: pl.BlockSpec(shape, lambda i: (0,) * len(shape))
    col = lambda v: v.reshape(-1, 1)
    return pl.pallas_call(
        functools.partial(_mla_proj_kernel, tq=tq, tk=tk),
        grid=(T // tm,),
        in_specs=[pl.BlockSpec((tm, D), lambda i: (i, 0)), full((1, D)),
                  full(winT.shape), full(wuqT.shape), full(wukvT.shape),
                  pl.BlockSpec((MLA_ROPE // 2, tm), lambda i: (0, i % nps)),
                  pl.BlockSpec((MLA_ROPE // 2, tm), lambda i: (0, i % nps)),
                  full((MLA_Q_LORA, 1)), full((MLA_KV_LORA, 1)), full((MLA_NOPE, 1)),
                  full((MLA_ROPE, 1)), full((MLA_NOPE, 1)), full((MLA_ROPE, 1))],
        out_specs=(pl.BlockSpec((tm // tq, MLA_HEADS * MLA_QK, tq), lambda i: (i, 0, 0)),
                   pl.BlockSpec((None, MLA_HEADS, tm, MLA_QK), lambda i: (i // nps, 0, i % nps, 0)),
                   pl.BlockSpec((tm // tk, MLA_HEADS * MLA_V, tk), lambda i: (i, 0, 0))),
        out_shape=(jax.ShapeDtypeStruct((T // tq, MLA_HEADS * MLA_QK, tq), MXU_DTYPE),
                   jax.ShapeDtypeStruct((B, MLA_HEADS, S, MLA_QK), MXU_DTYPE),
                   jax.ShapeDtypeStruct((T // tk, MLA_HEADS * MLA_V, tk), MXU_DTYPE)),
        compiler_params=_params("parallel"), name="mla_proj",
    )(x2, g.reshape(1, D), winT, wuqT, wukvT, cosT, sinT,
      col(q_lat_norm), col(kv_lat_norm), col(qn_norm), col(qr_norm), col(kn_norm), col(kr_norm))


def _mla_attn_kernel(q_ref, k_ref, v_ref, o_ref, s_scr, *, tq, tk):
    qi = pl.program_id(2)
    unroll = tq // tk
    nsub, _, tl = q_ref.shape
    qT = jnp.concatenate([q_ref[s] for s in range(nsub)], axis=1)
    diff = lax.broadcasted_iota(jnp.int32, (tk, 1), 0) - lax.broadcasted_iota(jnp.int32, (1, tq), 1)

    def scores(kt):
        return _dot(k_ref[pl.ds(pl.multiple_of(kt * tk, tk), tk), :], qT)

    _, l, acc = _flash_causal(qi, unroll, tk, scores, lambda kt: v_ref[kt], diff, s_scr,
                              _flash_init(MLA_V, tq))
    o = (acc / l).astype(o_ref.dtype)
    for s in range(nsub):
        o_ref[s] = o[:, s * tl:(s + 1) * tl]


def _mla_attn(qT, k, vT, B, S):
    tq, tk = MLA_TQ, MLA_TK
    tl = qT.shape[2]
    nsub = tq // tl
    nq = S // tq
    nkt = S // tk
    T = B * S
    return pl.pallas_call(
        functools.partial(_mla_attn_kernel, tq=tq, tk=tk),
        grid=(B, MLA_HEADS, nq),
        in_specs=[pl.BlockSpec((nsub, MLA_QK, tl), lambda b, h, i: (b * nq + i, h, 0)),
                  pl.BlockSpec((None, None, S, MLA_QK), lambda b, h, i: (b, h, 0, 0)),
                  pl.BlockSpec((nkt, MLA_V, tk), lambda b, h, i: (b, h, 0))],
        out_specs=pl.BlockSpec((nsub, MLA_V, tl), lambda b, h, i: (b * nq + i, h, 0)),
        out_shape=jax.ShapeDtypeStruct((T // tl, MLA_HEADS * MLA_V, tl), MXU_DTYPE),
        scratch_shapes=[pltpu.VMEM((tq // tk, tk, tq), F32)],
        compiler_params=_params("parallel", "parallel", "arbitrary"), name="mla_attn",
    )(qT, k, vT)


def _moe_router_kernel(x_ref, g_ref, w_ref, b_ref, id_ref, wt_ref, cnt_ref):
    tm = x_ref.shape[0]
    h = _rms_lanes(x_ref[...], g_ref[...]).astype(MXU_DTYPE)
    lg = _dot_nt(w_ref[...], h) + b_ref[...]
    gl = lg[:MOE_GROUPS]
    ge = jnp.exp(gl - jnp.max(gl, axis=0, keepdims=True))
    gp = ge / jnp.sum(ge, axis=0, keepdims=True)
    rid = lax.broadcasted_iota(jnp.int32, (MOE_GROUPS, 1), 0)
    g_w = jnp.max(gp, axis=0, keepdims=True)
    grp = jnp.min(jnp.where(gp == g_w, rid, MOE_GROUPS), axis=0, keepdims=True)
    el = jnp.zeros((MOE_EPG, tm), F32)
    for gi in range(MOE_GROUPS):
        rows = lg[MOE_GROUPS + gi * MOE_EPG:MOE_GROUPS + (gi + 1) * MOE_EPG]
        el = jnp.where(grp == gi, rows, el)
    ee = jnp.exp(el - jnp.max(el, axis=0, keepdims=True))
    ep = ee / jnp.sum(ee, axis=0, keepdims=True)
    p1 = jnp.max(ep, axis=0, keepdims=True)
    i1 = jnp.min(jnp.where(ep == p1, rid, MOE_EPG), axis=0, keepdims=True)
    ep2 = jnp.where(rid == i1, -1.0, ep)
    p2 = jnp.max(ep2, axis=0, keepdims=True)
    i2 = jnp.min(jnp.where(ep2 == p2, rid, MOE_EPG), axis=0, keepdims=True)
    den = p1 + p2
    zi = jnp.zeros((1, tm), jnp.int32)
    zf = jnp.zeros((1, tm), F32)
    e1 = grp * MOE_EPG + i1
    e2 = grp * MOE_EPG + i2
    id_ref[...] = jnp.concatenate([e1, e2] + [zi] * 6, axis=0)
    wt_ref[...] = jnp.concatenate([g_w * p1 / den, g_w * p2 / den] + [zf] * 6, axis=0)
    eid = lax.broadcasted_iota(jnp.int32, (MOE_EXPERTS, 1), 0)
    cnt_ref[...] = jnp.sum(((eid == e1) | (eid == e2)).astype(F32), axis=1, keepdims=True)


def _moe_router(x2, g, w_group, b_group, w_expert, b_expert):
    T, D = x2.shape
    tm = ROW_TILE
    nrow = 128
    pad = nrow - MOE_GROUPS - MOE_EXPERTS
    wT = jnp.concatenate([w_group, w_expert, jnp.zeros((D, pad), w_group.dtype)], axis=1).T.astype(MXU_DTYPE)
    bias = jnp.concatenate([b_group, b_expert, jnp.zeros((pad,), b_group.dtype)]).reshape(nrow, 1)
    return pl.pallas_call(
        _moe_router_kernel,
        grid=(T // tm,),
        in_specs=[pl.BlockSpec((tm, D), lambda i: (i, 0)), pl.BlockSpec((1, D), lambda i: (0, 0)),
                  pl.BlockSpec((nrow, D), lambda i: (0, 0)), pl.BlockSpec((nrow, 1), lambda i: (0, 0))],
        out_specs=(pl.BlockSpec((None, 8, tm), lambda i: (i, 0, 0)), pl.BlockSpec((None, 8, tm), lambda i: (i, 0, 0)),
                   pl.BlockSpec((None, MOE_EXPERTS, 1), lambda i: (i, 0, 0))),
        out_shape=(jax.ShapeDtypeStruct((T // tm, 8, tm), jnp.int32), jax.ShapeDtypeStruct((T // tm, 8, tm), F32),
                   jax.ShapeDtypeStruct((T // tm, MOE_EXPERTS, 1), F32)),
        compiler_params=_params("parallel"), name="moe_router",
    )(x2, g.reshape(1, D), wT, bias)


def _moe_plan_kernel(id_ref, base_ref, tri_ref, d_ref):
    tm = id_ref.shape[1]
    eid = lax.broadcasted_iota(jnp.int32, (MOE_EXPERTS, 1), 0)
    f0 = (eid == id_ref[0:1, :]).astype(F32)
    f1 = (eid == id_ref[1:2, :]).astype(F32)
    pre0 = _dot(f0.astype(MXU_DTYPE), tri_ref[...])
    pre1 = _dot(f1.astype(MXU_DTYPE), tri_ref[...])
    base = base_ref[...]
    tot0 = jnp.sum(f0, axis=1, keepdims=True)
    d0 = jnp.sum(f0 * (base + pre0), axis=0, keepdims=True)
    d1 = jnp.sum(f1 * (base + tot0 + pre1), axis=0, keepdims=True)
    zf = jnp.zeros((1, tm), F32)
    d_ref[...] = jnp.concatenate([d0, d1] + [zf] * 6, axis=0).astype(jnp.int32)


def _moe_plan(ids, cnt):
    nt, _, tm = ids.shape
    blk = MOE_BLOCK
    c = cnt.reshape(nt, MOE_EXPERTS)
    tot = jnp.sum(c, axis=0)
    padded = jnp.ceil(tot / blk) * blk
    pad_ends = jnp.cumsum(padded)
    tile_base = (pad_ends - padded)[None, :] + jnp.cumsum(c, axis=0) - c
    n_blocks = nt * tm * MOE_TOP_K // blk + MOE_EXPERTS
    first_row = jnp.arange(n_blocks, dtype=F32) * blk
    block_e = jnp.minimum(jnp.sum((pad_ends[None, :] <= first_row[:, None]).astype(jnp.int32), axis=1),
                          MOE_EXPERTS - 1)
    n_used = (pad_ends[-1] / blk).astype(jnp.int32).reshape(1)
    tri = jnp.asarray(np.arange(tm)[:, None] < np.arange(tm)[None, :], MXU_DTYPE)
    dest = pl.pallas_call(
        _moe_plan_kernel,
        grid=(nt,),
        in_specs=[pl.BlockSpec((None, 8, tm), lambda i: (i, 0, 0)),
                  pl.BlockSpec((None, MOE_EXPERTS, 1), lambda i: (i, 0, 0)),
                  pl.BlockSpec((tm, tm), lambda i: (0, 0))],
        out_specs=pl.BlockSpec((None, 8, tm), lambda i: (i, 0, 0)),
        out_shape=jax.ShapeDtypeStruct((nt, 8, tm), jnp.int32),
        compiler_params=_params("parallel"), name="moe_plan",
    )(ids, tile_base.reshape(nt, MOE_EXPERTS, 1), tri)
    return dest, block_e, n_used, n_blocks


def _moe_scatter_kernel(d_ref, x_ref, xs_zero, xs_hbm, sem):
    del xs_zero
    n = x_ref.shape[0]

    def start(j, c):
        src = x_ref.at[pl.ds(j, 1), :]
        for k in range(MOE_TOP_K):
            pltpu.make_async_copy(src, xs_hbm.at[pl.ds(d_ref[0, 0, k * n + j], 1), :], sem.at[0]).start()
        return c

    lax.fori_loop(0, n, start, 0, unroll=8)
    for k in range(MOE_TOP_K):
        pltpu.make_async_copy(x_ref, xs_hbm.at[pl.ds(0, n), :], sem.at[0]).wait()


def _moe_scatter(x2, dest, n_slots):
    T, D = x2.shape
    nt, _, tm = dest.shape
    d3 = dest[:, :MOE_TOP_K, :].reshape(nt, 1, MOE_TOP_K * tm)
    return pl.pallas_call(
        _moe_scatter_kernel,
        grid=(nt,),
        in_specs=[pl.BlockSpec((1, 1, MOE_TOP_K * tm), lambda i: (i, 0, 0), memory_space=pltpu.SMEM),
                  pl.BlockSpec((tm, D), lambda i: (i, 0)), pl.BlockSpec(memory_space=pl.ANY)],
        out_specs=pl.BlockSpec(memory_space=pl.ANY),
        out_shape=jax.ShapeDtypeStruct((n_slots, D), F32),
        scratch_shapes=[pltpu.SemaphoreType.DMA((1,))],
        input_output_aliases={2: 0},
        compiler_params=_params("arbitrary"), name="moe_scatter",
    )(d3, x2, jnp.zeros((n_slots, D), F32))


def _moe_expert_kernel(be_ref, nu_ref, x_ref, g_ref, wg_ref, wu_ref, wd_ref, y_ref):
    i = pl.program_id(0)

    @pl.when(i < nu_ref[0])
    def _():
        h = _rms_lanes(x_ref[...], g_ref[...]).astype(MXU_DTYPE)
        a = _dot(h, wg_ref[...])
        u = _dot(h, wu_ref[...])
        mid = (jax.nn.silu(a) * u).astype(MXU_DTYPE)
        y_ref[...] = _dot(mid, wd_ref[...])

    @pl.when(i >= nu_ref[0])
    def _():
        y_ref[...] = jnp.zeros_like(y_ref)


def _moe_experts(xs, g, block_e, n_used, w_gate, w_up, w_down):
    n_slots, D = xs.shape
    blk = MOE_BLOCK
    n_blocks = n_slots // blk
    grid_spec = pltpu.PrefetchScalarGridSpec(
        num_scalar_prefetch=2,
        grid=(n_blocks,),
        in_specs=[
            pl.BlockSpec((blk, D), lambda i, be, nu: (i, 0)),
            pl.BlockSpec((1, D), lambda i, be, nu: (0, 0)),
            pl.BlockSpec((None, D, MOE_FF), lambda i, be, nu: (be[i], 0, 0)),
            pl.BlockSpec((None, D, MOE_FF), lambda i, be, nu: (be[i], 0, 0)),
            pl.BlockSpec((None, MOE_FF, D), lambda i, be, nu: (be[i], 0, 0)),
        ],
        out_specs=pl.BlockSpec((blk, D), lambda i, be, nu: (i, 0)),
    )
    return pl.pallas_call(
        _moe_expert_kernel, grid_spec=grid_spec,
        out_shape=jax.ShapeDtypeStruct((n_slots, D), F32),
        compiler_params=_params("arbitrary"), name="moe_experts",
    )(block_e, n_used, xs, g.reshape(1, D), w_gate.astype(MXU_DTYPE), w_up.astype(MXU_DTYPE), w_down.astype(MXU_DTYPE))


def _moe_combine_kernel(d_ref, y_hbm, x_ref, w_ref, o_ref, ybuf, sem):
    tc = x_ref.shape[0]

    def start(r, c):
        pltpu.make_async_copy(y_hbm.at[pl.ds(d_ref[0, 0, r], 1), :], ybuf.at[pl.ds(r, 1), :], sem.at[0]).start()
        return c

    lax.fori_loop(0, MOE_TOP_K * tc, start, 0, unroll=8)
    pltpu.make_async_copy(y_hbm.at[pl.ds(0, MOE_TOP_K * tc), :], ybuf, sem.at[0]).wait()
    w = w_ref[...]
    o_ref[...] = x_ref[...] + w[:, 0:1] * ybuf[:tc, :] + w[:, 1:2] * ybuf[tc:, :]


def _moe_combine(x2, y, dest, wts):
    T, D = x2.shape
    nt, _, tm = dest.shape
    tc = MOE_TC
    d3 = dest[:, :MOE_TOP_K, :].reshape(nt, MOE_TOP_K, tm // tc, tc).transpose(0, 2, 1, 3).reshape(T // tc, 1, MOE_TOP_K * tc)
    w2 = wts[:, :MOE_TOP_K, :].transpose(0, 2, 1).reshape(T, MOE_TOP_K)
    return pl.pallas_call(
        _moe_combine_kernel,
        grid=(T // tc,),
        in_specs=[pl.BlockSpec((1, 1, MOE_TOP_K * tc), lambda i: (i, 0, 0), memory_space=pltpu.SMEM),
                  pl.BlockSpec(memory_space=pl.ANY),
                  pl.BlockSpec((tc, D), lambda i: (i, 0)),
                  pl.BlockSpec((tc, MOE_TOP_K), lambda i: (i, 0))],
        out_specs=pl.BlockSpec((tc, D), lambda i: (i, 0)),
        out_shape=jax.ShapeDtypeStruct((T, D), F32),
        scratch_shapes=[pltpu.VMEM((MOE_TOP_K * tc, D), F32), pltpu.SemaphoreType.DMA((1,))],
        compiler_params=_params("arbitrary"), name="moe_combine",
    )(d3, y, x2, w2)


def _hier_moe(x2, g, w_group, b_group, w_expert, b_expert, w_gate, w_up, w_down):
    ids, wts, cnt = _moe_router(x2, g, w_group, b_group, w_expert, b_expert)
    dest, block_e, n_used, n_blocks = _moe_plan(ids, cnt)
    xs = _moe_scatter(x2, dest, n_blocks * MOE_BLOCK)
    y = _moe_experts(xs, g, block_e, n_used, w_gate, w_up, w_down)
    return _moe_combine(x2, y, dest, wts)


def _rope_tables_T(seq, dim):
    inv = 1.0 / (ROPE_THETA ** (jnp.arange(0, dim, 2, dtype=F32) / dim))
    ang = jnp.arange(seq, dtype=F32)[:, None] * inv[None, :]
    return jnp.cos(ang).T, jnp.sin(ang).T


def _nsa_layer(x2, g, w_in, cmp_pos, cmp_w1, cmp_w2, q_norm, k_norm, w_out, cosT, sinT, B, S):
    qn, qr, ks, kw, vsT, vwT, gT, kc, vc = _nsa_proj(x2, g, w_in, q_norm, k_norm, cosT, sinT, B, S)
    kcmp = _compress(kc, cmp_pos[0], cmp_w1[0], cmp_w2[0], k_norm[0], B, S, True)
    vcmpT = _compress(vc, cmp_pos[1], cmp_w1[1], cmp_w2[1], k_norm[0], B, S, False)
    T = B * S
    gates5 = gT.reshape(T // NSA_TQ, 3, NSA_GROUPS, NSA_REP, NSA_TQ)
    ocg, selb = _nsa_cmp(qn, kcmp, vcmpT, gates5, B, S)
    aT = _nsa_attn(qr, ks, vsT, kw, vwT, selb, ocg, gates5, B, S)
    return _out_proj(aT, x2, w_out)


def _mla_layer(x2, g, w_in, q_lat_norm, kv_lat_norm, w_uq, w_ukv, qn_norm, qr_norm, kn_norm, kr_norm,
               w_out, cosT, sinT, B, S):
    qT, k, vT = _mla_proj(x2, g, w_in, q_lat_norm, kv_lat_norm, w_uq, w_ukv, qn_norm, qr_norm, kn_norm, kr_norm,
                          cosT, sinT, B, S)
    aT = _mla_attn(qT, k, vT, B, S)
    return _out_proj(aT, x2, w_out)


def kernel(x, norm_mix, norm_ffn, nsa_w_in, nsa_cmp_pos, nsa_cmp_w1, nsa_cmp_w2, nsa_q_norm, nsa_k_norm, nsa_w_out, mla_w_in, mla_q_lat_norm, mla_kv_lat_norm, mla_w_uq, mla_w_ukv, mla_q_nope_norm, mla_q_rope_norm, mla_k_nope_norm, mla_k_rope_norm, mla_w_out, moe_w_group, moe_b_group, moe_w_expert, moe_b_expert, moe_w_gate, moe_w_up, moe_w_down):
    B, S, D = x.shape
    depth = norm_mix.shape[0]
    assert S % max(ROW_TILE, NSA_TQ, MLA_TQ) == 0 and MLA_TQ % ROW_TILE == 0 and S >= WINDOW
    cosT, sinT = _rope_tables_T(S, NSA_HEAD_DIM)
    x2 = x.reshape(B * S, D)
    for i in range(depth):
        j = i // 2
        if i % 2 == 0:
            x2 = _nsa_layer(x2, norm_mix[i], nsa_w_in[j], nsa_cmp_pos[j], nsa_cmp_w1[j], nsa_cmp_w2[j],
                            nsa_q_norm[j], nsa_k_norm[j], nsa_w_out[j], cosT, sinT, B, S)
        else:
            x2 = _mla_layer(x2, norm_mix[i], mla_w_in[j], mla_q_lat_norm[j], mla_kv_lat_norm[j], mla_w_uq[j],
                            mla_w_ukv[j], mla_q_nope_norm[j], mla_q_rope_norm[j], mla_k_nope_norm[j],
                            mla_k_rope_norm[j], mla_w_out[j], cosT, sinT, B, S)
        x2 = _hier_moe(x2, norm_ffn[i], moe_w_group[i], moe_b_group[i], moe_w_expert[i], moe_b_expert[i],
                       moe_w_gate[i], moe_w_up[i], moe_w_down[i])
    return x2.reshape(B, S, D)
```

```python
import functools

import numpy as np
import jax
import jax.numpy as jnp
from jax import lax
from jax.experimental import pallas as pl
from jax.experimental.pallas import tpu as pltpu

RMS_EPS = 1e-6
ROPE_THETA = 10000.0
NEG_INF = -1e30

NSA_HEADS = 16
NSA_HEAD_DIM = 64
NSA_GROUPS = 4
NSA_REP = NSA_HEADS // NSA_GROUPS
CMP_LEN = 32
CMP_STRIDE = 16
CMP_HIDDEN = 128
SEL_LEN = 64
SEL_RATIO = SEL_LEN // CMP_STRIDE
SEL_TOPK = 16
SEL_FORCE = 1e4
WINDOW = 512
NSA_Q_DIM = NSA_HEADS * NSA_HEAD_DIM
NSA_KV_DIM = NSA_GROUPS * NSA_HEAD_DIM

MLA_HEADS = 8
MLA_NOPE = 128
MLA_ROPE = 64
MLA_V = 128
MLA_QK = MLA_NOPE + MLA_ROPE
MLA_Q_LORA = 256
MLA_KV_LORA = 256

MOE_GROUPS = 8
MOE_EPG = 8
MOE_EXPERTS = MOE_GROUPS * MOE_EPG
MOE_TOP_K = 2
MOE_FF = 256

MXU_DTYPE = jnp.bfloat16
ROW_TILE = 512
NSA_TQ = 512
NSA_TK = 256
MLA_TQ = 1024
MLA_TK = 256
V_PAD = 16
MOE_BLOCK = 256
MOE_TC = 256
VMEM_LIMIT = 56 * 1024 * 1024

_NT = (((1,), (1,)), ((), ()))
F32 = jnp.float32
LOG2E = 1.4426950408889634


def _params(*sem):
    return pltpu.CompilerParams(dimension_semantics=sem, vmem_limit_bytes=VMEM_LIMIT)


def _dot(a, b):
    return jnp.dot(a, b, preferred_element_type=F32)


def _dot_nt(a, b):
    return lax.dot_general(a, b, _NT, preferred_element_type=F32)


def _rms_lanes(x, g):
    ms = jnp.mean(x * x, axis=-1, keepdims=True)
    return x * lax.rsqrt(ms + RMS_EPS) * g


def _rms_rows(y, g):
    ms = jnp.mean(y * y, axis=-2, keepdims=True)
    return y * lax.rsqrt(ms + RMS_EPS) * g


def _rope_rows(y, cos, sin):
    half = y.shape[-2] // 2
    y1 = y[..., :half, :]
    y2 = y[..., half:, :]
    return jnp.concatenate([y1 * cos - y2 * sin, y2 * cos + y1 * sin], axis=-2)


def _nsa_proj_kernel(x_ref, g_ref, wq_ref, wk_ref, wv_ref, wg_ref, wc_ref, cos_ref, sin_ref,
                     qg_ref, kg_ref, blk_ref,
                     qn_ref, qr_ref, ks_ref, kw_ref, vs_ref, vw_ref, gt_ref, kc_ref, vc_ref,
                     *, tq, tk):
    tm = x_ref.shape[0]
    h = _rms_lanes(x_ref[...], g_ref[...]).astype(MXU_DTYPE)
    cos = cos_ref[...]
    sin = sin_ref[...]
    scale = NSA_HEAD_DIM ** -0.5 * LOG2E

    q3 = _dot_nt(wq_ref[...], h).reshape(NSA_HEADS, NSA_HEAD_DIM, tm)
    qn = _rms_rows(q3, qg_ref[...])
    qr = _rope_rows(qn, cos, sin)
    qn = (qn * scale).astype(MXU_DTYPE).reshape(NSA_Q_DIM, tm)
    qr = (qr * scale).astype(MXU_DTYPE).reshape(NSA_Q_DIM, tm)
    for s in range(tm // tq):
        qn_ref[s] = qn[:, s * tq:(s + 1) * tq]
        qr_ref[s] = qr[:, s * tq:(s + 1) * tq]

    k4 = _dot_nt(wk_ref[...], h).reshape(2, NSA_GROUPS, NSA_HEAD_DIM, tm)
    for br, o_ref in ((0, ks_ref), (1, kw_ref)):
        kb = _rope_rows(_rms_rows(k4[br], kg_ref[br]), cos, sin)
        kb = kb.reshape(NSA_KV_DIM, tm).T.astype(MXU_DTYPE)
        for g in range(NSA_GROUPS):
            kg = kb[:, g * NSA_HEAD_DIM:(g + 1) * NSA_HEAD_DIM]
            o_ref[g] = jnp.concatenate([kg, blk_ref[...]], axis=1) if br == 0 else kg

    vT = _dot_nt(wv_ref[...], h).astype(MXU_DTYPE)
    for br, o_ref in ((0, vs_ref), (1, vw_ref)):
        va = jnp.concatenate([_append_ones(vT[(br * NSA_GROUPS + g) * NSA_HEAD_DIM:
                                              (br * NSA_GROUPS + g + 1) * NSA_HEAD_DIM])
                              for g in range(NSA_GROUPS)], axis=0)
        for s in range(tm // tk):
            o_ref[s] = va[:, s * tk:(s + 1) * tk]

    gT = jax.nn.sigmoid(_dot_nt(wg_ref[...], h))
    for s in range(tm // tq):
        gt_ref[s] = gT[:, s * tq:(s + 1) * tq]

    c = _dot(h, wc_ref[...])
    kc_ref[...] = c[:, :NSA_KV_DIM]
    vc_ref[...] = c[:, NSA_KV_DIM:]


def _nsa_proj(x2, g, w_in, q_norm, k_norm, cosT, sinT, B, S):
    T, D = x2.shape
    tm, tq, tk = ROW_TILE, NSA_TQ, NSA_TK
    nps = S // tm
    o = NSA_Q_DIM
    kvd = NSA_KV_DIM
    wq = w_in[:, :o].T.astype(MXU_DTYPE)
    wc = w_in[:, o:o + 2 * kvd].astype(MXU_DTYPE)
    wk = jnp.concatenate([w_in[:, o + 2 * kvd:o + 3 * kvd], w_in[:, o + 4 * kvd:o + 5 * kvd]], axis=1).T.astype(MXU_DTYPE)
    wv = jnp.concatenate([w_in[:, o + 3 * kvd:o + 4 * kvd], w_in[:, o + 5 * kvd:o + 6 * kvd]], axis=1).T.astype(MXU_DTYPE)
    wg = w_in[:, o + 6 * kvd:].T.astype(MXU_DTYPE)
    ng = wg.shape[0]
    nsel = S // SEL_LEN
    vrows = NSA_GROUPS * (NSA_HEAD_DIM + V_PAD)
    assert NSA_HEAD_DIM + nsel <= 256
    blk_onehot = jnp.asarray(np.arange(S)[:, None] // SEL_LEN == np.arange(nsel)[None, :], MXU_DTYPE)
    full = lambda shape: pl.BlockSpec(shape, lambda i: (0,) * len(shape))
    out_shapes = (
        jax.ShapeDtypeStruct((T // tq, o, tq), MXU_DTYPE),
        jax.ShapeDtypeStruct((T // tq, o, tq), MXU_DTYPE),
        jax.ShapeDtypeStruct((B, NSA_GROUPS, S, NSA_HEAD_DIM + nsel), MXU_DTYPE),
        jax.ShapeDtypeStruct((B, NSA_GROUPS, S, NSA_HEAD_DIM), MXU_DTYPE),
        jax.ShapeDtypeStruct((T // tk, vrows, tk), MXU_DTYPE),
        jax.ShapeDtypeStruct((T // tk, vrows, tk), MXU_DTYPE),
        jax.ShapeDtypeStruct((T // tq, ng, tq), F32),
        jax.ShapeDtypeStruct((T, kvd), F32),
        jax.ShapeDtypeStruct((T, kvd), F32),
    )
    kspec = pl.BlockSpec((None, NSA_GROUPS, tm, NSA_HEAD_DIM), lambda i: (i // nps, 0, i % nps, 0))
    kaspec = pl.BlockSpec((None, NSA_GROUPS, tm, NSA_HEAD_DIM + nsel), lambda i: (i // nps, 0, i % nps, 0))
    out_specs = (
        pl.BlockSpec((tm // tq, o, tq), lambda i: (i, 0, 0)),
        pl.BlockSpec((tm // tq, o, tq), lambda i: (i, 0, 0)),
        kaspec, kspec,
        pl.BlockSpec((tm // tk, vrows, tk), lambda i: (i, 0, 0)),
        pl.BlockSpec((tm // tk, vrows, tk), lambda i: (i, 0, 0)),
        pl.BlockSpec((tm // tq, ng, tq), lambda i: (i, 0, 0)),
        pl.BlockSpec((tm, kvd), lambda i: (i, 0)),
        pl.BlockSpec((tm, kvd), lambda i: (i, 0)),
    )
    in_specs = [
        pl.BlockSpec((tm, D), lambda i: (i, 0)),
        full((1, D)), full(wq.shape), full(wk.shape), full(wv.shape), full(wg.shape), full(wc.shape),
        pl.BlockSpec((NSA_HEAD_DIM // 2, tm), lambda i: (0, i % nps)),
        pl.BlockSpec((NSA_HEAD_DIM // 2, tm), lambda i: (0, i % nps)),
        full((NSA_HEAD_DIM, 1)), full((2, NSA_HEAD_DIM, 1)),
        pl.BlockSpec((tm, nsel), lambda i: (i % nps, 0)),
    ]
    return pl.pallas_call(
        functools.partial(_nsa_proj_kernel, tq=tq, tk=tk),
        grid=(T // tm,), in_specs=in_specs, out_specs=out_specs, out_shape=out_shapes,
        compiler_params=_params("parallel"), name="nsa_proj",
    )(x2, g.reshape(1, D), wq, wk, wv, wg, wc, cosT, sinT,
      q_norm.reshape(NSA_HEAD_DIM, 1), k_norm[1:3].reshape(2, NSA_HEAD_DIM, 1), blk_onehot)


def _compress_kernel(x_ref, pa_ref, pb_ref, w1a_ref, w1b_ref, w2t_ref, gain_ref, o_ref, *, normalize, kchunk):
    n, kdim = x_ref.shape
    hdim = w1a_ref.shape[1]
    h1 = jnp.zeros((n, hdim), F32)
    h2 = jnp.zeros((n, hdim), F32)
    for c in range(kdim // kchunk):
        sl = slice(c * kchunk, (c + 1) * kchunk)
        xs = x_ref[:, sl]
        h1 = h1 + _dot((xs + pa_ref[:, sl]).astype(MXU_DTYPE), w1a_ref[sl, :])
        h2 = h2 + _dot((xs + pb_ref[:, sl]).astype(MXU_DTYPE), w1b_ref[sl, :])
    hid = h1 + pltpu.roll(h2, n - 1, axis=0)
    act = jax.nn.gelu(hid).astype(MXU_DTYPE)
    oT = _dot_nt(w2t_ref[...], act)
    if normalize:
        o3 = _rms_rows(oT.reshape(NSA_GROUPS, NSA_HEAD_DIM, n), gain_ref[...])
        o = o3.reshape(NSA_KV_DIM, n).T.astype(MXU_DTYPE)
        for g in range(NSA_GROUPS):
            o_ref[g] = o[:, g * NSA_HEAD_DIM:(g + 1) * NSA_HEAD_DIM]
    else:
        o_ref[...] = oT.astype(MXU_DTYPE)


def _compress_weights(pos, w1, w2):
    G, Dh, H = NSA_GROUPS, NSA_HEAD_DIM, CMP_HIDDEN
    eye = jnp.eye(G, dtype=w1.dtype)

    def expand(w1_half):
        w = w1_half.reshape(CMP_STRIDE, Dh, H)
        return jnp.einsum('rdj,gk->rgdkj', w, eye).reshape(CMP_STRIDE * G * Dh, G * H).astype(MXU_DTYPE)

    half = CMP_STRIDE * Dh
    w1a, w1b = expand(w1[:half]), expand(w1[half:])
    tilepos = lambda p: jnp.broadcast_to(p[:, None, :], (CMP_STRIDE, G, Dh)).reshape(1, CMP_STRIDE * G * Dh)
    pa, pb = tilepos(pos[:CMP_STRIDE]), tilepos(pos[CMP_STRIDE:])
    w2t = jnp.einsum('jd,gk->kdgj', w2, eye).reshape(G * Dh, G * H).astype(MXU_DTYPE)
    return pa, pb, w1a, w1b, w2t


def _compress(xc, pos, w1, w2, gain, B, S, normalize):
    n = S // CMP_STRIDE
    kdim = CMP_STRIDE * NSA_KV_DIM
    xr = xc.reshape(B * n, kdim)
    pa, pb, w1a, w1b, w2t = _compress_weights(pos, w1, w2)
    full = lambda shape: pl.BlockSpec(shape, lambda b: (0,) * len(shape))
    if normalize:
        out_shape = jax.ShapeDtypeStruct((B, NSA_GROUPS, n, NSA_HEAD_DIM), MXU_DTYPE)
        out_spec = pl.BlockSpec((None, NSA_GROUPS, n, NSA_HEAD_DIM), lambda b: (b, 0, 0, 0))
    else:
        out_shape = jax.ShapeDtypeStruct((B, NSA_KV_DIM, n), MXU_DTYPE)
        out_spec = pl.BlockSpec((None, NSA_KV_DIM, n), lambda b: (b, 0, 0))
    return pl.pallas_call(
        functools.partial(_compress_kernel, normalize=normalize, kchunk=1024),
        grid=(B,),
        in_specs=[pl.BlockSpec((n, kdim), lambda b: (b, 0)), full(pa.shape), full(pb.shape),
                  full(w1a.shape), full(w1b.shape), full(w2t.shape), full((NSA_HEAD_DIM, 1))],
        out_specs=out_spec, out_shape=out_shape,
        compiler_params=_params("parallel"), name="nsa_compress",
    )(xr, pa, pb, w1a, w1b, w2t, gain.reshape(NSA_HEAD_DIM, 1))


def _stack_heads(ref):
    return jnp.concatenate([ref[r * NSA_HEAD_DIM:(r + 1) * NSA_HEAD_DIM, :] for r in range(NSA_REP)], axis=1)


def _nsa_cmp_kernel(qn_ref, kc_ref, vct_ref, wimp_ref, gt_ref, oc_ref, sel_ref, *, tq, sel_k, nq, nvar):
    qi = pl.program_id(2)
    for v in range(nvar):
        nc = kc_ref.shape[0] * (v + 1) // nvar
        ns = sel_ref.shape[0] * (v + 1) // nvar

        @pl.when((qi >= v * nq // nvar) & (qi < (v + 1) * nq // nvar))
        def _(nc=nc, ns=ns):
            _nsa_cmp_body(qn_ref, kc_ref, vct_ref, wimp_ref, gt_ref, oc_ref, sel_ref, qi, tq, sel_k, nc, ns)


def _nsa_cmp_body(qn_ref, kc_ref, vct_ref, wimp_ref, gt_ref, oc_ref, sel_ref, qi, tq, sel_k, ncmp, nsel):
    q0 = qi * tq
    lanes = NSA_REP * tq
    qT = _stack_heads(qn_ref)
    s = _dot(kc_ref[:ncmp, :], qT)
    t4 = q0 + lax.broadcasted_iota(jnp.int32, (1, lanes), 1) % tq
    cmp_end = lax.broadcasted_iota(jnp.int32, (ncmp, 1), 0) * CMP_STRIDE + (CMP_LEN - 1)
    mask = cmp_end <= t4
    s = jnp.where(mask, s, NEG_INF)
    m = jnp.max(s, axis=0, keepdims=True)
    e = jnp.where(mask, jnp.exp2(s - m), 0.0)
    p = e / jnp.maximum(jnp.sum(e, axis=0, keepdims=True), 1e-30)
    ocT = _dot(vct_ref[:, :ncmp], p.astype(MXU_DTYPE))
    for r in range(NSA_REP):
        oc_ref[r * NSA_HEAD_DIM:(r + 1) * NSA_HEAD_DIM, :] = ocT[:, r * tq:(r + 1) * tq] * gt_ref[0, r:r + 1, :]

    psum = p[:, :tq]
    for r in range(1, NSA_REP):
        psum = psum + p[:, r * tq:(r + 1) * tq]
    hi = psum.astype(MXU_DTYPE)
    lo = (psum - hi.astype(F32)).astype(MXU_DTYPE)
    wimp = wimp_ref[:nsel, :ncmp]
    imp = _dot(wimp, hi) + _dot(wimp, lo)

    t = q0 + lax.broadcasted_iota(jnp.int32, (1, tq), 1)
    j = lax.broadcasted_iota(jnp.int32, (nsel, 1), 0)
    cur = t // SEL_LEN
    valid = j * SEL_LEN <= t
    forced = (j == 0) | (j == cur) | (j == cur - 1)
    score = jnp.where(forced, SEL_FORCE, jnp.where(valid, imp, -1.0))
    jf = j.astype(F32)
    sel = jnp.zeros((nsel, tq), jnp.bool_)
    for _ in range(sel_k):
        mx = jnp.max(score, axis=0, keepdims=True)
        first = jnp.min(jnp.where(score == mx, jf, float(nsel)), axis=0, keepdims=True)
        hit = jf == first
        sel = sel | hit
        score = jnp.where(hit, -jnp.inf, score)
    sel_ref[:nsel, :] = jnp.where(sel, 0.0, NEG_INF).astype(sel_ref.dtype)
    if nsel < sel_ref.shape[0]:
        sel_ref[nsel:, :] = jnp.full((sel_ref.shape[0] - nsel, tq), NEG_INF, sel_ref.dtype)


def _imp_weights(S):
    nsel, ncmp_pad = S // SEL_LEN, S // CMP_STRIDE
    ncmp = ncmp_pad - (CMP_LEN // CMP_STRIDE) + 1
    w = np.zeros((nsel, ncmp_pad), np.float32)
    for jj in range(nsel):
        for off in range(-(CMP_LEN // CMP_STRIDE - 1), SEL_RATIO):
            i = SEL_RATIO * jj + off
            lo = max(off * CMP_STRIDE, 0)
            hi = min(off * CMP_STRIDE + CMP_LEN - 1, SEL_LEN - 1)
            if 0 <= i < ncmp:
                w[jj, i] = (hi - lo + 1) / CMP_STRIDE
    return jnp.asarray(w, MXU_DTYPE)


def _nsa_cmp(qn, kcmp, vcmpT, gates5, B, S):
    tq = NSA_TQ
    nq = S // tq
    ncmp = S // CMP_STRIDE
    nsel = S // SEL_LEN
    wimp = _imp_weights(S)
    T = B * S
    qdim = NSA_REP * NSA_HEAD_DIM
    nvar = max([v for v in (4, 2) if nq % v == 0 and ncmp % (128 * v) == 0 and nsel % (16 * v) == 0
                and nsel // v >= SEL_TOPK] + [1])
    return pl.pallas_call(
        functools.partial(_nsa_cmp_kernel, tq=tq, sel_k=min(SEL_TOPK, nsel), nq=nq, nvar=nvar),
        grid=(B, NSA_GROUPS, nq),
        in_specs=[
            pl.BlockSpec((None, qdim, tq), lambda b, g, i: (b * nq + i, g, 0)),
            pl.BlockSpec((None, None, ncmp, NSA_HEAD_DIM), lambda b, g, i: (b, g, 0, 0)),
            pl.BlockSpec((None, NSA_HEAD_DIM, ncmp), lambda b, g, i: (b, g, 0)),
            pl.BlockSpec((nsel, ncmp), lambda b, g, i: (0, 0)),
            pl.BlockSpec((None, 3, None, NSA_REP, tq), lambda b, g, i: (b * nq + i, 0, g, 0, 0)),
        ],
        out_specs=(
            pl.BlockSpec((None, qdim, tq), lambda b, g, i: (b * nq + i, g, 0)),
            pl.BlockSpec((None, None, None, nsel, tq), lambda b, g, i: (b, g, i, 0, 0)),
        ),
        out_shape=(
            jax.ShapeDtypeStruct((T // tq, NSA_Q_DIM, tq), F32),
            jax.ShapeDtypeStruct((B, NSA_GROUPS, nq, nsel, tq), MXU_DTYPE),
        ),
        compiler_params=_params("parallel", "parallel", "parallel"), name="nsa_cmp_select",
    )(qn, kcmp, vcmpT, wimp, gates5)


def _flash_step(carry, s, vT):
    m, acc = carry
    m_new = jnp.maximum(m, jnp.max(s, axis=0, keepdims=True))
    alpha = jnp.exp2(m - m_new)
    p = jnp.exp2(s - m_new).astype(MXU_DTYPE)
    return m_new, alpha * acc + _dot(vT, p)


def _flash_init(dv, nq):
    return (jnp.full((1, nq), NEG_INF, F32), jnp.zeros((dv + V_PAD, nq), F32))


def _flash_finish(carry, dv):
    _, acc = carry
    return acc[:dv] / acc[dv:dv + 1]


def _append_ones(vT):
    n = vT.shape[1]
    ones = (lax.broadcasted_iota(jnp.int32, (V_PAD, n), 0) == 0).astype(vT.dtype)
    return jnp.concatenate([vT, ones], axis=0)


def _flash_causal(qi, unroll, tk, scores, values, diff, s_scr, carry):
    for u in range(unroll):
        s_scr[u] = scores(u)

    def body(i, carry):
        for u in range(unroll):
            m, acc = carry
            s = s_scr[u]
            m_new = jnp.maximum(m, jnp.max(s, axis=0, keepdims=True))
            alpha = jnp.exp2(m - m_new)
            p = jnp.exp2(s - m_new).astype(MXU_DTYPE)
            s_scr[u] = scores((i + 1) * unroll + u)
            carry = (m_new, alpha * acc + _dot(values(i * unroll + u), p))
        return carry

    carry = lax.fori_loop(0, qi, body, carry)
    for d in range(unroll):
        s = jnp.where(diff <= -d * tk, s_scr[d], NEG_INF)
        carry = _flash_step(carry, s, values(qi * unroll + d))
    return carry


def _nsa_attn_kernel(qr_ref, ks_ref, vs_ref, kw_ref, vw_ref, sel_ref, oc_ref, gt_ref, o_ref, s_scr, *, tq, tk):
    qi = pl.program_id(2)
    lanes = NSA_REP * tq
    dh = NSA_HEAD_DIM
    unroll = tq // tk
    qT = _stack_heads(qr_ref)
    q_aug = jnp.concatenate([qT, jnp.concatenate([sel_ref[...]] * NSA_REP, axis=1)], axis=0)
    tl = lax.broadcasted_iota(jnp.int32, (1, lanes), 1) % tq
    diff = lax.broadcasted_iota(jnp.int32, (tk, 1), 0) - tl

    def win_scores(d):
        kt = qi * unroll + d
        mask = diff <= -d * tk if d >= 0 else None
        lo_thr = -d * tk - WINDOW
        if d < 0:
            lo = diff > jnp.where(kt >= 0, lo_thr, 2 ** 30)
            kt = jnp.maximum(kt, 0)
        else:
            lo = diff > lo_thr if lo_thr >= -(tq - 1) else None
        if lo is not None:
            mask = lo if mask is None else mask & lo
        s = _dot(kw_ref[pl.ds(pl.multiple_of(kt * tk, tk), tk), :], qT)
        return s if mask is None else jnp.where(mask, s, NEG_INF)

    order = [0] + list(range(1, unroll)) + list(range(-(WINDOW // tk), 0))
    carry = _flash_init(dh, lanes)
    nxt = win_scores(order[0])
    for i, d in enumerate(order):
        m, acc = carry
        s = nxt
        m_new = jnp.maximum(m, jnp.max(s, axis=0, keepdims=True))
        alpha = jnp.exp2(m - m_new)
        p = jnp.exp2(s - m_new).astype(MXU_DTYPE)
        if i + 1 < len(order):
            nxt = win_scores(order[i + 1])
        carry = (m_new, alpha * acc + _dot(vw_ref[jnp.maximum(qi * unroll + d, 0)], p))
    o_w = _flash_finish(carry, dh)

    def sel_scores(kt):
        return _dot(ks_ref[pl.ds(pl.multiple_of(kt * tk, tk), tk), :], q_aug)

    o_s = _flash_finish(_flash_causal(qi, unroll, tk, sel_scores, lambda kt: vs_ref[kt], diff, s_scr,
                                      _flash_init(dh, lanes)), dh)
    for r in range(NSA_REP):
        ls = slice(r * tq, (r + 1) * tq)
        rs = slice(r * dh, (r + 1) * dh)
        o = oc_ref[rs, :] + gt_ref[1, r:r + 1, :] * o_s[:, ls] + gt_ref[2, r:r + 1, :] * o_w[:, ls]
        o_ref[rs, :] = o.astype(o_ref.dtype)


def _nsa_attn(qr, ks, vsT, kw, vwT, selb, ocg, gates5, B, S):
    tq, tk = NSA_TQ, NSA_TK
    nq = S // tq
    nkt = S // tk
    nsel = S // SEL_LEN
    T = B * S
    qdim = NSA_REP * NSA_HEAD_DIM
    assert tq % tk == 0 and WINDOW % tk == 0 and tq <= WINDOW
    kspec = pl.BlockSpec((None, None, S, NSA_HEAD_DIM), lambda b, g, i: (b, g, 0, 0))
    kaspec = pl.BlockSpec((None, None, S, NSA_HEAD_DIM + nsel), lambda b, g, i: (b, g, 0, 0))
    vspec = pl.BlockSpec((nkt, NSA_HEAD_DIM + V_PAD, tk), lambda b, g, i: (b, g, 0))
    qspec = pl.BlockSpec((None, qdim, tq), lambda b, g, i: (b * nq + i, g, 0))
    return pl.pallas_call(
        functools.partial(_nsa_attn_kernel, tq=tq, tk=tk),
        grid=(B, NSA_GROUPS, nq),
        in_specs=[
            qspec, kaspec, vspec, kspec, vspec,
            pl.BlockSpec((None, None, None, nsel, tq), lambda b, g, i: (b, g, i, 0, 0)),
            qspec,
            pl.BlockSpec((None, 3, None, NSA_REP, tq), lambda b, g, i: (b * nq + i, 0, g, 0, 0)),
        ],
        out_specs=qspec,
        out_shape=jax.ShapeDtypeStruct((T // tq, NSA_Q_DIM, tq), MXU_DTYPE),
        scratch_shapes=[pltpu.VMEM((tq // tk, tk, NSA_REP * tq), F32)],
        compiler_params=_params("parallel", "parallel", "arbitrary"), name="nsa_attn",
    )(qr, ks, vsT, kw, vwT, selb, ocg, gates5)


def _out_proj_kernel(a_ref, x_ref, w_ref, o_ref):
    nsub, _, tq = a_ref.shape
    for s in range(nsub):
        yT = _dot(w_ref[...], a_ref[s])
        o_ref[s * tq:(s + 1) * tq, :] = x_ref[s * tq:(s + 1) * tq, :] + yT.T


def _out_proj(aT, x2, w_out):
    T, D = x2.shape
    nt, feat, tq = aT.shape
    tm = ROW_TILE
    wT = w_out.T.astype(MXU_DTYPE)
    return pl.pallas_call(
        _out_proj_kernel,
        grid=(T // tm,),
        in_specs=[pl.BlockSpec((tm // tq, feat, tq), lambda i: (i, 0, 0)),
                  pl.BlockSpec((tm, D), lambda i: (i, 0)),
                  pl.BlockSpec((D, feat), lambda i: (0, 0))],
        out_specs=pl.BlockSpec((tm, D), lambda i: (i, 0)),
        out_shape=jax.ShapeDtypeStruct((T, D), F32),
        compiler_params=_params("parallel"), name="out_proj",
    )(aT, x2, wT)


def _mla_proj_kernel(x_ref, g_ref, win_ref, wuq_ref, wukv_ref, cos_ref, sin_ref,
                     gq_ref, gkv_ref, gqn_ref, gqr_ref, gkn_ref, gkr_ref,
                     q_ref, k_ref, v_ref, *, tq, tk):
    tm = x_ref.shape[0]
    h = _rms_lanes(x_ref[...], g_ref[...]).astype(MXU_DTYPE)
    cos = cos_ref[...]
    sin = sin_ref[...]
    cT = _dot_nt(win_ref[...], h)
    cq = _rms_rows(cT[:MLA_Q_LORA], gq_ref[...]).astype(MXU_DTYPE)
    ckv = _rms_rows(cT[MLA_Q_LORA:MLA_Q_LORA + MLA_KV_LORA], gkv_ref[...]).astype(MXU_DTYPE)
    kr = _rope_rows(_rms_rows(cT[MLA_Q_LORA + MLA_KV_LORA:], gkr_ref[...]), cos, sin)

    scale = MLA_QK ** -0.5 * LOG2E
    q3 = _dot(wuq_ref[...], cq).reshape(MLA_HEADS, MLA_QK, tm)
    qn = _rms_rows(q3[:, :MLA_NOPE, :], gqn_ref[...])
    qr = _rope_rows(_rms_rows(q3[:, MLA_NOPE:, :], gqr_ref[...]), cos, sin)
    q = (jnp.concatenate([qn, qr], axis=1) * scale).astype(MXU_DTYPE).reshape(MLA_HEADS * MLA_QK, tm)
    for s in range(tm // tq):
        q_ref[s] = q[:, s * tq:(s + 1) * tq]

    kv3 = _dot(wukv_ref[...], ckv).reshape(MLA_HEADS, MLA_NOPE + MLA_V, tm)
    v3 = kv3[:, MLA_NOPE:, :].astype(MXU_DTYPE)
    vT = jnp.concatenate([_append_ones(v3[hh]) for hh in range(MLA_HEADS)], axis=0)
    for s in range(tm // tk):
        v_ref[s] = vT[:, s * tk:(s + 1) * tk]
    kn = _rms_rows(kv3[:, :MLA_NOPE, :], gkn_ref[...])
    kr_t = kr.T.astype(MXU_DTYPE)
    for hh in range(MLA_HEADS):
        k_ref[hh] = jnp.concatenate([kn[hh].T.astype(MXU_DTYPE), kr_t], axis=1)


def _mla_proj(x2, g, w_in, q_lat_norm, kv_lat_norm, w_uq, w_ukv, qn_norm, qr_norm, kn_norm, kr_norm,
              cosT, sinT, B, S):
    T, D = x2.shape
    tm, tq, tk = ROW_TILE, ROW_TILE, MLA_TK
    nps = S // tm
    winT = w_in.T.astype(MXU_DTYPE)
    wuqT = w_uq.T.astype(MXU_DTYPE)
    wukvT = w_ukv.T.astype(MXU_DTYPE)
    full = lambda shape: pl.BlockSpec(shape, lambda i: (0,) * len(shape))
    col = lambda v: v.reshape(-1, 1)
    return pl.pallas_call(
        functools.partial(_mla_proj_kernel, tq=tq, tk=tk),
        grid=(T // tm,),
        in_specs=[pl.BlockSpec((tm, D), lambda i: (i, 0)), full((1, D)),
                  full(winT.shape), full(wuqT.shape), full(wukvT.shape),
                  pl.BlockSpec((MLA_ROPE // 2, tm), lambda i: (0, i % nps)),
                  pl.BlockSpec((MLA_ROPE // 2, tm), lambda i: (0, i % nps)),
                  full((MLA_Q_LORA, 1)), full((MLA_KV_LORA, 1)), full((MLA_NOPE, 1)),
                  full((MLA_ROPE, 1)), full((MLA_NOPE, 1)), full((MLA_ROPE, 1))],
        out_specs=(pl.BlockSpec((tm // tq, MLA_HEADS * MLA_QK, tq), lambda i: (i, 0, 0)),
                   pl.BlockSpec((None, MLA_HEADS, tm, MLA_QK), lambda i: (i // nps, 0, i % nps, 0)),
                   pl.BlockSpec((tm // tk, MLA_HEADS * (MLA_V + V_PAD), tk), lambda i: (i, 0, 0))),
        out_shape=(jax.ShapeDtypeStruct((T // tq, MLA_HEADS * MLA_QK, tq), MXU_DTYPE),
                   jax.ShapeDtypeStruct((B, MLA_HEADS, S, MLA_QK), MXU_DTYPE),
                   jax.ShapeDtypeStruct((T // tk, MLA_HEADS * (MLA_V + V_PAD), tk), MXU_DTYPE)),
        compiler_params=_params("parallel"), name="mla_proj",
    )(x2, g.reshape(1, D), winT, wuqT, wukvT, cosT, sinT,
      col(q_lat_norm), col(kv_lat_norm), col(qn_norm), col(qr_norm), col(kn_norm), col(kr_norm))


def _mla_attn_kernel(q_ref, k_ref, v_ref, o_ref, s_scr, *, tq, tk):
    qi = pl.program_id(2)
    unroll = tq // tk
    nsub, _, tl = q_ref.shape
    qT = jnp.concatenate([q_ref[s] for s in range(nsub)], axis=1)
    diff = lax.broadcasted_iota(jnp.int32, (tk, 1), 0) - lax.broadcasted_iota(jnp.int32, (1, tq), 1)

    def scores(kt):
        return _dot(k_ref[pl.ds(pl.multiple_of(kt * tk, tk), tk), :], qT)

    o = _flash_finish(_flash_causal(qi, unroll, tk, scores, lambda kt: v_ref[kt], diff, s_scr,
                                    _flash_init(MLA_V, tq)), MLA_V).astype(o_ref.dtype)
    for s in range(nsub):
        o_ref[s] = o[:, s * tl:(s + 1) * tl]


def _mla_attn(qT, k, vT, B, S):
    tq, tk = MLA_TQ, MLA_TK
    tl = qT.shape[2]
    nsub = tq // tl
    nq = S // tq
    nkt = S // tk
    T = B * S
    return pl.pallas_call(
        functools.partial(_mla_attn_kernel, tq=tq, tk=tk),
        grid=(B, MLA_HEADS, nq),
        in_specs=[pl.BlockSpec((nsub, MLA_QK, tl), lambda b, h, i: (b * nq + i, h, 0)),
                  pl.BlockSpec((None, None, S, MLA_QK), lambda b, h, i: (b, h, 0, 0)),
                  pl.BlockSpec((nkt, MLA_V + V_PAD, tk), lambda b, h, i: (b, h, 0))],
        out_specs=pl.BlockSpec((nsub, MLA_V, tl), lambda b, h, i: (b * nq + i, h, 0)),
        out_shape=jax.ShapeDtypeStruct((T // tl, MLA_HEADS * MLA_V, tl), MXU_DTYPE),
        scratch_shapes=[pltpu.VMEM((tq // tk, tk, tq), F32)],
        compiler_params=_params("parallel", "parallel", "arbitrary"), name="mla_attn",
    )(qT, k, vT)


def _moe_router_kernel(x_ref, g_ref, w_ref, b_ref, id_ref, wt_ref, cnt_ref):
    tm = x_ref.shape[0]
    h = _rms_lanes(x_ref[...], g_ref[...]).astype(MXU_DTYPE)
    lg = _dot_nt(w_ref[...], h) + b_ref[...]
    gl = lg[:MOE_GROUPS]
    ge = jnp.exp(gl - jnp.max(gl, axis=0, keepdims=True))
    gp = ge / jnp.sum(ge, axis=0, keepdims=True)
    rid = lax.broadcasted_iota(jnp.int32, (MOE_GROUPS, 1), 0)
    g_w = jnp.max(gp, axis=0, keepdims=True)
    grp = jnp.min(jnp.where(gp == g_w, rid, MOE_GROUPS), axis=0, keepdims=True)
    el = jnp.zeros((MOE_EPG, tm), F32)
    for gi in range(MOE_GROUPS):
        rows = lg[MOE_GROUPS + gi * MOE_EPG:MOE_GROUPS + (gi + 1) * MOE_EPG]
        el = jnp.where(grp == gi, rows, el)
    ee = jnp.exp(el - jnp.max(el, axis=0, keepdims=True))
    ep = ee / jnp.sum(ee, axis=0, keepdims=True)
    p1 = jnp.max(ep, axis=0, keepdims=True)
    i1 = jnp.min(jnp.where(ep == p1, rid, MOE_EPG), axis=0, keepdims=True)
    ep2 = jnp.where(rid == i1, -1.0, ep)
    p2 = jnp.max(ep2, axis=0, keepdims=True)
    i2 = jnp.min(jnp.where(ep2 == p2, rid, MOE_EPG), axis=0, keepdims=True)
    den = p1 + p2
    zi = jnp.zeros((1, tm), jnp.int32)
    zf = jnp.zeros((1, tm), F32)
    e1 = grp * MOE_EPG + i1
    e2 = grp * MOE_EPG + i2
    id_ref[...] = jnp.concatenate([e1, e2] + [zi] * 6, axis=0)
    wt_ref[...] = jnp.concatenate([g_w * p1 / den, g_w * p2 / den] + [zf] * 6, axis=0)
    eid = lax.broadcasted_iota(jnp.int32, (MOE_EXPERTS, 1), 0)
    cnt_ref[...] = jnp.sum(((eid == e1) | (eid == e2)).astype(F32), axis=1, keepdims=True)


def _moe_router(x2, g, w_group, b_group, w_expert, b_expert):
    T, D = x2.shape
    tm = ROW_TILE
    nrow = 128
    pad = nrow - MOE_GROUPS - MOE_EXPERTS
    wT = jnp.concatenate([w_group, w_expert, jnp.zeros((D, pad), w_group.dtype)], axis=1).T.astype(MXU_DTYPE)
    bias = jnp.concatenate([b_group, b_expert, jnp.zeros((pad,), b_group.dtype)]).reshape(nrow, 1)
    return pl.pallas_call(
        _moe_router_kernel,
        grid=(T // tm,),
        in_specs=[pl.BlockSpec((tm, D), lambda i: (i, 0)), pl.BlockSpec((1, D), lambda i: (0, 0)),
                  pl.BlockSpec((nrow, D), lambda i: (0, 0)), pl.BlockSpec((nrow, 1), lambda i: (0, 0))],
        out_specs=(pl.BlockSpec((None, 8, tm), lambda i: (i, 0, 0)), pl.BlockSpec((None, 8, tm), lambda i: (i, 0, 0)),
                   pl.BlockSpec((None, MOE_EXPERTS, 1), lambda i: (i, 0, 0))),
        out_shape=(jax.ShapeDtypeStruct((T // tm, 8, tm), jnp.int32), jax.ShapeDtypeStruct((T // tm, 8, tm), F32),
                   jax.ShapeDtypeStruct((T // tm, MOE_EXPERTS, 1), F32)),
        compiler_params=_params("parallel"), name="moe_router",
    )(x2, g.reshape(1, D), wT, bias)


def _moe_plan_kernel(id_ref, base_ref, tri_ref, d_ref):
    tm = id_ref.shape[1]
    eid = lax.broadcasted_iota(jnp.int32, (MOE_EXPERTS, 1), 0)
    f0 = (eid == id_ref[0:1, :]).astype(F32)
    f1 = (eid == id_ref[1:2, :]).astype(F32)
    pre0 = _dot(f0.astype(MXU_DTYPE), tri_ref[...])
    pre1 = _dot(f1.astype(MXU_DTYPE), tri_ref[...])
    base = base_ref[...]
    tot0 = jnp.sum(f0, axis=1, keepdims=True)
    d0 = jnp.sum(f0 * (base + pre0), axis=0, keepdims=True)
    d1 = jnp.sum(f1 * (base + tot0 + pre1), axis=0, keepdims=True)
    zf = jnp.zeros((1, tm), F32)
    d_ref[...] = jnp.concatenate([d0, d1] + [zf] * 6, axis=0).astype(jnp.int32)


def _moe_plan(ids, cnt):
    nt, _, tm = ids.shape
    blk = MOE_BLOCK
    c = cnt.reshape(nt, MOE_EXPERTS)
    tot = jnp.sum(c, axis=0)
    padded = jnp.ceil(tot / blk) * blk
    pad_ends = jnp.cumsum(padded)
    tile_base = (pad_ends - padded)[None, :] + jnp.cumsum(c, axis=0) - c
    n_blocks = nt * tm * MOE_TOP_K // blk + MOE_EXPERTS
    first_row = jnp.arange(n_blocks, dtype=F32) * blk
    block_e = jnp.minimum(jnp.sum((pad_ends[None, :] <= first_row[:, None]).astype(jnp.int32), axis=1),
                          MOE_EXPERTS - 1)
    n_used = (pad_ends[-1] / blk).astype(jnp.int32).reshape(1)
    tri = jnp.asarray(np.arange(tm)[:, None] < np.arange(tm)[None, :], MXU_DTYPE)
    dest = pl.pallas_call(
        _moe_plan_kernel,
        grid=(nt,),
        in_specs=[pl.BlockSpec((None, 8, tm), lambda i: (i, 0, 0)),
                  pl.BlockSpec((None, MOE_EXPERTS, 1), lambda i: (i, 0, 0)),
                  pl.BlockSpec((tm, tm), lambda i: (0, 0))],
        out_specs=pl.BlockSpec((None, 8, tm), lambda i: (i, 0, 0)),
        out_shape=jax.ShapeDtypeStruct((nt, 8, tm), jnp.int32),
        compiler_params=_params("parallel"), name="moe_plan",
    )(ids, tile_base.reshape(nt, MOE_EXPERTS, 1), tri)
    return dest, block_e, n_used, n_blocks


def _moe_scatter_kernel(d_ref, x_ref, xs_prev, xs_hbm, sem):
    del xs_prev
    n = x_ref.shape[0]

    def start(j, c):
        src = x_ref.at[pl.ds(j, 1), :]
        for k in range(MOE_TOP_K):
            pltpu.make_async_copy(src, xs_hbm.at[pl.ds(d_ref[0, 0, k * n + j], 1), :], sem.at[0]).start()
        return c

    lax.fori_loop(0, n, start, 0, unroll=8)
    for k in range(MOE_TOP_K):
        pltpu.make_async_copy(x_ref, xs_hbm.at[pl.ds(0, n), :], sem.at[0]).wait()


def _moe_scatter(x2, dest, xs_buf):
    T, D = x2.shape
    n_slots = xs_buf.shape[0]
    nt, _, tm = dest.shape
    d3 = dest[:, :MOE_TOP_K, :].reshape(nt, 1, MOE_TOP_K * tm)
    return pl.pallas_call(
        _moe_scatter_kernel,
        grid=(nt,),
        in_specs=[pl.BlockSpec((1, 1, MOE_TOP_K * tm), lambda i: (i, 0, 0), memory_space=pltpu.SMEM),
                  pl.BlockSpec((tm, D), lambda i: (i, 0)), pl.BlockSpec(memory_space=pl.ANY)],
        out_specs=pl.BlockSpec(memory_space=pl.ANY),
        out_shape=jax.ShapeDtypeStruct((n_slots, D), F32),
        scratch_shapes=[pltpu.SemaphoreType.DMA((1,))],
        input_output_aliases={2: 0},
        compiler_params=_params("arbitrary"), name="moe_scatter",
    )(d3, x2, xs_buf)


def _moe_expert_kernel(be_ref, nu_ref, x_ref, g_ref, wg_ref, wu_ref, wd_ref, y_ref, wg_s, wu_s, wd_s):
    i = pl.program_id(0)

    @pl.when((i == 0) | (be_ref[i] != be_ref[jnp.maximum(i - 1, 0)]))
    def _():
        wg_s[...] = wg_ref[...].astype(MXU_DTYPE)
        wu_s[...] = wu_ref[...].astype(MXU_DTYPE)
        wd_s[...] = wd_ref[...].astype(MXU_DTYPE)

    @pl.when(i < nu_ref[0])
    def _():
        h = _rms_lanes(x_ref[...], g_ref[...]).astype(MXU_DTYPE)
        a = _dot(h, wg_s[...])
        u = _dot(h, wu_s[...])
        mid = (jax.nn.silu(a) * u).astype(MXU_DTYPE)
        y_ref[...] = _dot(mid, wd_s[...])

    @pl.when(i >= nu_ref[0])
    def _():
        y_ref[...] = jnp.zeros_like(y_ref)


def _moe_experts(xs, g, block_e, n_used, w_gate, w_up, w_down):
    n_slots, D = xs.shape
    blk = MOE_BLOCK
    n_blocks = n_slots // blk
    grid_spec = pltpu.PrefetchScalarGridSpec(
        num_scalar_prefetch=2,
        grid=(n_blocks,),
        in_specs=[
            pl.BlockSpec((blk, D), lambda i, be, nu: (i, 0)),
            pl.BlockSpec((1, D), lambda i, be, nu: (0, 0)),
            pl.BlockSpec((None, D, MOE_FF), lambda i, be, nu: (be[i], 0, 0)),
            pl.BlockSpec((None, D, MOE_FF), lambda i, be, nu: (be[i], 0, 0)),
            pl.BlockSpec((None, MOE_FF, D), lambda i, be, nu: (be[i], 0, 0)),
        ],
        out_specs=pl.BlockSpec((blk, D), lambda i, be, nu: (i, 0)),
        scratch_shapes=[pltpu.VMEM((D, MOE_FF), MXU_DTYPE), pltpu.VMEM((D, MOE_FF), MXU_DTYPE),
                        pltpu.VMEM((MOE_FF, D), MXU_DTYPE)],
    )
    return pl.pallas_call(
        _moe_expert_kernel, grid_spec=grid_spec,
        out_shape=jax.ShapeDtypeStruct((n_slots, D), F32),
        compiler_params=_params("arbitrary"), name="moe_experts",
    )(block_e, n_used, xs, g.reshape(1, D), w_gate, w_up, w_down)


def _moe_combine_kernel(d_ref, y_hbm, x_ref, w_ref, o_ref, ybuf, sem):
    tc = x_ref.shape[0]

    def start(r, c):
        pltpu.make_async_copy(y_hbm.at[pl.ds(d_ref[0, 0, r], 1), :], ybuf.at[pl.ds(r, 1), :], sem.at[0]).start()
        return c

    lax.fori_loop(0, MOE_TOP_K * tc, start, 0, unroll=8)
    pltpu.make_async_copy(y_hbm.at[pl.ds(0, MOE_TOP_K * tc), :], ybuf, sem.at[0]).wait()
    w = w_ref[...]
    o_ref[...] = x_ref[...] + w[:, 0:1] * ybuf[:tc, :] + w[:, 1:2] * ybuf[tc:, :]


def _moe_combine(x2, y, dest, wts):
    T, D = x2.shape
    nt, _, tm = dest.shape
    tc = MOE_TC
    d3 = dest[:, :MOE_TOP_K, :].reshape(nt, MOE_TOP_K, tm // tc, tc).transpose(0, 2, 1, 3).reshape(T // tc, 1, MOE_TOP_K * tc)
    w2 = wts[:, :MOE_TOP_K, :].transpose(0, 2, 1).reshape(T, MOE_TOP_K)
    return pl.pallas_call(
        _moe_combine_kernel,
        grid=(T // tc,),
        in_specs=[pl.BlockSpec((1, 1, MOE_TOP_K * tc), lambda i: (i, 0, 0), memory_space=pltpu.SMEM),
                  pl.BlockSpec(memory_space=pl.ANY),
                  pl.BlockSpec((tc, D), lambda i: (i, 0)),
                  pl.BlockSpec((tc, MOE_TOP_K), lambda i: (i, 0))],
        out_specs=pl.BlockSpec((tc, D), lambda i: (i, 0)),
        out_shape=jax.ShapeDtypeStruct((T, D), F32),
        scratch_shapes=[pltpu.VMEM((MOE_TOP_K * tc, D), F32), pltpu.SemaphoreType.DMA((1,))],
        compiler_params=_params("arbitrary"), name="moe_combine",
    )(d3, y, x2, w2)


def _hier_moe(x2, g, w_group, b_group, w_expert, b_expert, w_gate, w_up, w_down, xs_buf):
    ids, wts, cnt = _moe_router(x2, g, w_group, b_group, w_expert, b_expert)
    dest, block_e, n_used, n_blocks = _moe_plan(ids, cnt)
    assert xs_buf.shape[0] == n_blocks * MOE_BLOCK
    xs = _moe_scatter(x2, dest, xs_buf)
    y = _moe_experts(xs, g, block_e, n_used, w_gate, w_up, w_down)
    return _moe_combine(x2, y, dest, wts), xs


def _rope_tables_T(seq, dim):
    inv = 1.0 / (ROPE_THETA ** (jnp.arange(0, dim, 2, dtype=F32) / dim))
    ang = jnp.arange(seq, dtype=F32)[:, None] * inv[None, :]
    return jnp.cos(ang).T, jnp.sin(ang).T


def _nsa_layer(x2, g, w_in, cmp_pos, cmp_w1, cmp_w2, q_norm, k_norm, w_out, cosT, sinT, B, S):
    qn, qr, ks, kw, vsT, vwT, gT, kc, vc = _nsa_proj(x2, g, w_in, q_norm, k_norm, cosT, sinT, B, S)
    kcmp = _compress(kc, cmp_pos[0], cmp_w1[0], cmp_w2[0], k_norm[0], B, S, True)
    vcmpT = _compress(vc, cmp_pos[1], cmp_w1[1], cmp_w2[1], k_norm[0], B, S, False)
    T = B * S
    gates5 = gT.reshape(T // NSA_TQ, 3, NSA_GROUPS, NSA_REP, NSA_TQ)
    ocg, selb = _nsa_cmp(qn, kcmp, vcmpT, gates5, B, S)
    aT = _nsa_attn(qr, ks, vsT, kw, vwT, selb, ocg, gates5, B, S)
    return _out_proj(aT, x2, w_out)


def _mla_layer(x2, g, w_in, q_lat_norm, kv_lat_norm, w_uq, w_ukv, qn_norm, qr_norm, kn_norm, kr_norm,
               w_out, cosT, sinT, B, S):
    qT, k, vT = _mla_proj(x2, g, w_in, q_lat_norm, kv_lat_norm, w_uq, w_ukv, qn_norm, qr_norm, kn_norm, kr_norm,
                          cosT, sinT, B, S)
    aT = _mla_attn(qT, k, vT, B, S)
    return _out_proj(aT, x2, w_out)


def kernel(x, norm_mix, norm_ffn, nsa_w_in, nsa_cmp_pos, nsa_cmp_w1, nsa_cmp_w2, nsa_q_norm, nsa_k_norm, nsa_w_out, mla_w_in, mla_q_lat_norm, mla_kv_lat_norm, mla_w_uq, mla_w_ukv, mla_q_nope_norm, mla_q_rope_norm, mla_k_nope_norm, mla_k_rope_norm, mla_w_out, moe_w_group, moe_b_group, moe_w_expert, moe_b_expert, moe_w_gate, moe_w_up, moe_w_down):
    B, S, D = x.shape
    depth = norm_mix.shape[0]
    assert S % max(ROW_TILE, NSA_TQ, MLA_TQ) == 0 and MLA_TQ % ROW_TILE == 0 and S >= WINDOW
    cosT, sinT = _rope_tables_T(S, NSA_HEAD_DIM)
    x2 = x.reshape(B * S, D)
    xs_buf = jnp.zeros((B * S * MOE_TOP_K + MOE_EXPERTS * MOE_BLOCK, D), F32)
    for i in range(depth):
        j = i // 2
        if i % 2 == 0:
            x2 = _nsa_layer(x2, norm_mix[i], nsa_w_in[j], nsa_cmp_pos[j], nsa_cmp_w1[j], nsa_cmp_w2[j],
                            nsa_q_norm[j], nsa_k_norm[j], nsa_w_out[j], cosT, sinT, B, S)
        else:
            x2 = _mla_layer(x2, norm_mix[i], mla_w_in[j], mla_q_lat_norm[j], mla_kv_lat_norm[j], mla_w_uq[j],
                            mla_w_ukv[j], mla_q_nope_norm[j], mla_q_rope_norm[j], mla_k_nope_norm[j],
                            mla_k_rope_norm[j], mla_w_out[j], cosT, sinT, B, S)
        x2, xs_buf = _hier_moe(x2, norm_ffn[i], moe_w_group[i], moe_b_group[i], moe_w_expert[i], moe_b_expert[i],
                               moe_w_gate[i], moe_w_up[i], moe_w_down[i], xs_buf)
    return x2.reshape(B, S, D)
```

```python
import functools

import numpy as np
import jax
import jax.numpy as jnp
from jax import lax
from jax.experimental import pallas as pl
from jax.experimental.pallas import tpu as pltpu

RMS_EPS = 1e-6
ROPE_THETA = 10000.0
NEG_INF = -1e30

NSA_HEADS = 16
NSA_HEAD_DIM = 64
NSA_GROUPS = 4
NSA_REP = NSA_HEADS // NSA_GROUPS
CMP_LEN = 32
CMP_STRIDE = 16
CMP_HIDDEN = 128
SEL_LEN = 64
SEL_RATIO = SEL_LEN // CMP_STRIDE
SEL_TOPK = 16
SEL_FORCE = 1e4
WINDOW = 512
NSA_Q_DIM = NSA_HEADS * NSA_HEAD_DIM
NSA_KV_DIM = NSA_GROUPS * NSA_HEAD_DIM

MLA_HEADS = 8
MLA_NOPE = 128
MLA_ROPE = 64
MLA_V = 128
MLA_QK = MLA_NOPE + MLA_ROPE
MLA_Q_LORA = 256
MLA_KV_LORA = 256

MOE_GROUPS = 8
MOE_EPG = 8
MOE_EXPERTS = MOE_GROUPS * MOE_EPG
MOE_TOP_K = 2
MOE_FF = 256

MXU_DTYPE = jnp.bfloat16
ROW_TILE = 512
NSA_TQ = 512
NSA_TK = 256
MLA_TQ = 1024
MLA_TK = 256
V_PAD = 16
MOE_BLOCK = 256
MOE_TC = 256
VMEM_LIMIT = 56 * 1024 * 1024

_NT = (((1,), (1,)), ((), ()))
F32 = jnp.float32
LOG2E = 1.4426950408889634


def _params(*sem):
    return pltpu.CompilerParams(dimension_semantics=sem, vmem_limit_bytes=VMEM_LIMIT)


def _dot(a, b):
    return jnp.dot(a, b, preferred_element_type=F32)


def _dot_nt(a, b):
    return lax.dot_general(a, b, _NT, preferred_element_type=F32)


def _rms_lanes(x, g):
    ms = jnp.mean(x * x, axis=-1, keepdims=True)
    return x * lax.rsqrt(ms + RMS_EPS) * g


def _rms_rows(y, g):
    ms = jnp.mean(y * y, axis=-2, keepdims=True)
    return y * lax.rsqrt(ms + RMS_EPS) * g


def _rope_rows(y, cos, sin):
    half = y.shape[-2] // 2
    y1 = y[..., :half, :]
    y2 = y[..., half:, :]
    return jnp.concatenate([y1 * cos - y2 * sin, y2 * cos + y1 * sin], axis=-2)


def _nsa_proj_kernel(x_ref, g_ref, wq_ref, wk_ref, wv_ref, wg_ref, wc_ref, cos_ref, sin_ref,
                     qg_ref, kg_ref, blk_ref,
                     qn_ref, qr_ref, ks_ref, kw_ref, vs_ref, vw_ref, gt_ref, kc_ref, vc_ref,
                     *, tq, tk):
    tm = x_ref.shape[0]
    h = _rms_lanes(x_ref[...], g_ref[...]).astype(MXU_DTYPE)
    cos = cos_ref[...]
    sin = sin_ref[...]
    scale = NSA_HEAD_DIM ** -0.5 * LOG2E

    q3 = _dot_nt(wq_ref[...], h).reshape(NSA_HEADS, NSA_HEAD_DIM, tm)
    qn = _rms_rows(q3, qg_ref[...])
    qr = _rope_rows(qn, cos, sin)
    qn = (qn * scale).astype(MXU_DTYPE).reshape(NSA_Q_DIM, tm)
    qr = (qr * scale).astype(MXU_DTYPE).reshape(NSA_Q_DIM, tm)
    for s in range(tm // tq):
        qn_ref[s] = qn[:, s * tq:(s + 1) * tq]
        qr_ref[s] = qr[:, s * tq:(s + 1) * tq]

    k4 = _dot_nt(wk_ref[...], h).reshape(2, NSA_GROUPS, NSA_HEAD_DIM, tm)
    for br, o_ref in ((0, ks_ref), (1, kw_ref)):
        kb = _rope_rows(_rms_rows(k4[br], kg_ref[br]), cos, sin)
        kb = kb.reshape(NSA_KV_DIM, tm).T.astype(MXU_DTYPE)
        for g in range(NSA_GROUPS):
            kg = kb[:, g * NSA_HEAD_DIM:(g + 1) * NSA_HEAD_DIM]
            o_ref[g] = jnp.concatenate([kg, blk_ref[...]], axis=1) if br == 0 else kg

    vT = _dot_nt(wv_ref[...], h).astype(MXU_DTYPE)
    for br, o_ref in ((0, vs_ref), (1, vw_ref)):
        va = jnp.concatenate([_append_ones(vT[(br * NSA_GROUPS + g) * NSA_HEAD_DIM:
                                              (br * NSA_GROUPS + g + 1) * NSA_HEAD_DIM])
                              for g in range(NSA_GROUPS)], axis=0)
        for s in range(tm // tk):
            o_ref[s] = va[:, s * tk:(s + 1) * tk]

    gT = jax.nn.sigmoid(_dot_nt(wg_ref[...], h))
    for s in range(tm // tq):
        gt_ref[s] = gT[:, s * tq:(s + 1) * tq]

    c = _dot(h, wc_ref[...])
    kc_ref[...] = c[:, :NSA_KV_DIM]
    vc_ref[...] = c[:, NSA_KV_DIM:]


def _nsa_proj(x2, g, w_in, q_norm, k_norm, cosT, sinT, B, S):
    T, D = x2.shape
    tm, tq, tk = ROW_TILE, NSA_TQ, NSA_TK
    nps = S // tm
    o = NSA_Q_DIM
    kvd = NSA_KV_DIM
    wq = w_in[:, :o].T.astype(MXU_DTYPE)
    wc = w_in[:, o:o + 2 * kvd].astype(MXU_DTYPE)
    wk = jnp.concatenate([w_in[:, o + 2 * kvd:o + 3 * kvd], w_in[:, o + 4 * kvd:o + 5 * kvd]], axis=1).T.astype(MXU_DTYPE)
    wv = jnp.concatenate([w_in[:, o + 3 * kvd:o + 4 * kvd], w_in[:, o + 5 * kvd:o + 6 * kvd]], axis=1).T.astype(MXU_DTYPE)
    wg = w_in[:, o + 6 * kvd:].T.astype(MXU_DTYPE)
    ng = wg.shape[0]
    nsel = S // SEL_LEN
    vrows = NSA_GROUPS * (NSA_HEAD_DIM + V_PAD)
    assert NSA_HEAD_DIM + nsel <= 256
    blk_onehot = jnp.asarray(np.arange(S)[:, None] // SEL_LEN == np.arange(nsel)[None, :], MXU_DTYPE)
    full = lambda shape: pl.BlockSpec(shape, lambda i: (0,) * len(shape))
    out_shapes = (
        jax.ShapeDtypeStruct((T // tq, o, tq), MXU_DTYPE),
        jax.ShapeDtypeStruct((T // tq, o, tq), MXU_DTYPE),
        jax.ShapeDtypeStruct((B, NSA_GROUPS, S, NSA_HEAD_DIM + nsel), MXU_DTYPE),
        jax.ShapeDtypeStruct((B, NSA_GROUPS, S, NSA_HEAD_DIM), MXU_DTYPE),
        jax.ShapeDtypeStruct((T // tk, vrows, tk), MXU_DTYPE),
        jax.ShapeDtypeStruct((T // tk, vrows, tk), MXU_DTYPE),
        jax.ShapeDtypeStruct((T // tq, ng, tq), F32),
        jax.ShapeDtypeStruct((T, kvd), F32),
        jax.ShapeDtypeStruct((T, kvd), F32),
    )
    kspec = pl.BlockSpec((None, NSA_GROUPS, tm, NSA_HEAD_DIM), lambda i: (i // nps, 0, i % nps, 0))
    kaspec = pl.BlockSpec((None, NSA_GROUPS, tm, NSA_HEAD_DIM + nsel), lambda i: (i // nps, 0, i % nps, 0))
    out_specs = (
        pl.BlockSpec((tm // tq, o, tq), lambda i: (i, 0, 0)),
        pl.BlockSpec((tm // tq, o, tq), lambda i: (i, 0, 0)),
        kaspec, kspec,
        pl.BlockSpec((tm // tk, vrows, tk), lambda i: (i, 0, 0)),
        pl.BlockSpec((tm // tk, vrows, tk), lambda i: (i, 0, 0)),
        pl.BlockSpec((tm // tq, ng, tq), lambda i: (i, 0, 0)),
        pl.BlockSpec((tm, kvd), lambda i: (i, 0)),
        pl.BlockSpec((tm, kvd), lambda i: (i, 0)),
    )
    in_specs = [
        pl.BlockSpec((tm, D), lambda i: (i, 0)),
        full((1, D)), full(wq.shape), full(wk.shape), full(wv.shape), full(wg.shape), full(wc.shape),
        pl.BlockSpec((NSA_HEAD_DIM // 2, tm), lambda i: (0, i % nps)),
        pl.BlockSpec((NSA_HEAD_DIM // 2, tm), lambda i: (0, i % nps)),
        full((NSA_HEAD_DIM, 1)), full((2, NSA_HEAD_DIM, 1)),
        pl.BlockSpec((tm, nsel), lambda i: (i % nps, 0)),
    ]
    return pl.pallas_call(
        functools.partial(_nsa_proj_kernel, tq=tq, tk=tk),
        grid=(T // tm,), in_specs=in_specs, out_specs=out_specs, out_shape=out_shapes,
        compiler_params=_params("parallel"), name="nsa_proj",
    )(x2, g.reshape(1, D), wq, wk, wv, wg, wc, cosT, sinT,
      q_norm.reshape(NSA_HEAD_DIM, 1), k_norm[1:3].reshape(2, NSA_HEAD_DIM, 1), blk_onehot)


def _compress_kernel(x_ref, pa_ref, pb_ref, w1a_ref, w1b_ref, w2t_ref, gain_ref, o_ref, *, normalize, kchunk):
    n, kdim = x_ref.shape
    hdim = w1a_ref.shape[1]
    h1 = jnp.zeros((n, hdim), F32)
    h2 = jnp.zeros((n, hdim), F32)
    for c in range(kdim // kchunk):
        sl = slice(c * kchunk, (c + 1) * kchunk)
        xs = x_ref[:, sl]
        h1 = h1 + _dot((xs + pa_ref[:, sl]).astype(MXU_DTYPE), w1a_ref[sl, :])
        h2 = h2 + _dot((xs + pb_ref[:, sl]).astype(MXU_DTYPE), w1b_ref[sl, :])
    hid = h1 + pltpu.roll(h2, n - 1, axis=0)
    act = jax.nn.gelu(hid).astype(MXU_DTYPE)
    oT = _dot_nt(w2t_ref[...], act)
    if normalize:
        o3 = _rms_rows(oT.reshape(NSA_GROUPS, NSA_HEAD_DIM, n), gain_ref[...])
        o = o3.reshape(NSA_KV_DIM, n).T.astype(MXU_DTYPE)
        for g in range(NSA_GROUPS):
            o_ref[g] = o[:, g * NSA_HEAD_DIM:(g + 1) * NSA_HEAD_DIM]
    else:
        o_ref[...] = oT.astype(MXU_DTYPE)


def _compress_weights(pos, w1, w2):
    G, Dh, H = NSA_GROUPS, NSA_HEAD_DIM, CMP_HIDDEN
    eye = jnp.eye(G, dtype=w1.dtype)

    def expand(w1_half):
        w = w1_half.reshape(CMP_STRIDE, Dh, H)
        return jnp.einsum('rdj,gk->rgdkj', w, eye).reshape(CMP_STRIDE * G * Dh, G * H).astype(MXU_DTYPE)

    half = CMP_STRIDE * Dh
    w1a, w1b = expand(w1[:half]), expand(w1[half:])
    tilepos = lambda p: jnp.broadcast_to(p[:, None, :], (CMP_STRIDE, G, Dh)).reshape(1, CMP_STRIDE * G * Dh)
    pa, pb = tilepos(pos[:CMP_STRIDE]), tilepos(pos[CMP_STRIDE:])
    w2t = jnp.einsum('jd,gk->kdgj', w2, eye).reshape(G * Dh, G * H).astype(MXU_DTYPE)
    return pa, pb, w1a, w1b, w2t


def _compress(xc, pos, w1, w2, gain, B, S, normalize):
    n = S // CMP_STRIDE
    kdim = CMP_STRIDE * NSA_KV_DIM
    xr = xc.reshape(B * n, kdim)
    pa, pb, w1a, w1b, w2t = _compress_weights(pos, w1, w2)
    full = lambda shape: pl.BlockSpec(shape, lambda b: (0,) * len(shape))
    if normalize:
        out_shape = jax.ShapeDtypeStruct((B, NSA_GROUPS, n, NSA_HEAD_DIM), MXU_DTYPE)
        out_spec = pl.BlockSpec((None, NSA_GROUPS, n, NSA_HEAD_DIM), lambda b: (b, 0, 0, 0))
    else:
        out_shape = jax.ShapeDtypeStruct((B, NSA_KV_DIM, n), MXU_DTYPE)
        out_spec = pl.BlockSpec((None, NSA_KV_DIM, n), lambda b: (b, 0, 0))
    return pl.pallas_call(
        functools.partial(_compress_kernel, normalize=normalize, kchunk=1024),
        grid=(B,),
        in_specs=[pl.BlockSpec((n, kdim), lambda b: (b, 0)), full(pa.shape), full(pb.shape),
                  full(w1a.shape), full(w1b.shape), full(w2t.shape), full((NSA_HEAD_DIM, 1))],
        out_specs=out_spec, out_shape=out_shape,
        compiler_params=_params("parallel"), name="nsa_compress",
    )(xr, pa, pb, w1a, w1b, w2t, gain.reshape(NSA_HEAD_DIM, 1))


def _stack_heads(ref):
    return jnp.concatenate([ref[r * NSA_HEAD_DIM:(r + 1) * NSA_HEAD_DIM, :] for r in range(NSA_REP)], axis=1)


def _nsa_cmp_kernel(qn_ref, kc_ref, vct_ref, wimp_ref, gt_ref, oc_ref, sel_ref, *, tq, sel_k, nq, nvar):
    qi = pl.program_id(2)
    for v in range(nvar):
        nc = kc_ref.shape[0] * (v + 1) // nvar
        ns = sel_ref.shape[0] * (v + 1) // nvar

        @pl.when((qi >= v * nq // nvar) & (qi < (v + 1) * nq // nvar))
        def _(nc=nc, ns=ns):
            _nsa_cmp_body(qn_ref, kc_ref, vct_ref, wimp_ref, gt_ref, oc_ref, sel_ref, qi, tq, sel_k, nc, ns)


def _nsa_cmp_body(qn_ref, kc_ref, vct_ref, wimp_ref, gt_ref, oc_ref, sel_ref, qi, tq, sel_k, ncmp, nsel):
    q0 = qi * tq
    lanes = NSA_REP * tq
    qT = _stack_heads(qn_ref)
    s = _dot(kc_ref[:ncmp, :], qT)
    t4 = q0 + lax.broadcasted_iota(jnp.int32, (1, lanes), 1) % tq
    cmp_end = lax.broadcasted_iota(jnp.int32, (ncmp, 1), 0) * CMP_STRIDE + (CMP_LEN - 1)
    mask = cmp_end <= t4
    s = jnp.where(mask, s, NEG_INF)
    m = jnp.max(s, axis=0, keepdims=True)
    e = jnp.where(mask, jnp.exp2(s - m), 0.0)
    p = e / jnp.maximum(jnp.sum(e, axis=0, keepdims=True), 1e-30)
    ocT = _dot(vct_ref[:, :ncmp], p.astype(MXU_DTYPE))
    for r in range(NSA_REP):
        oc_ref[r * NSA_HEAD_DIM:(r + 1) * NSA_HEAD_DIM, :] = ocT[:, r * tq:(r + 1) * tq] * gt_ref[0, r:r + 1, :]

    psum = p[:, :tq]
    for r in range(1, NSA_REP):
        psum = psum + p[:, r * tq:(r + 1) * tq]
    hi = psum.astype(MXU_DTYPE)
    lo = (psum - hi.astype(F32)).astype(MXU_DTYPE)
    wimp = wimp_ref[:nsel, :ncmp]
    imp = _dot(wimp, hi) + _dot(wimp, lo)

    t = q0 + lax.broadcasted_iota(jnp.int32, (1, tq), 1)
    j = lax.broadcasted_iota(jnp.int32, (nsel, 1), 0)
    cur = t // SEL_LEN
    valid = j * SEL_LEN <= t
    forced = (j == 0) | (j == cur) | (j == cur - 1)
    score = jnp.where(forced, SEL_FORCE, jnp.where(valid, imp, -1.0))
    jf = j.astype(F32)
    sel = jnp.zeros((nsel, tq), jnp.bool_)
    for _ in range(sel_k):
        mx = jnp.max(score, axis=0, keepdims=True)
        first = jnp.min(jnp.where(score == mx, jf, float(nsel)), axis=0, keepdims=True)
        hit = jf == first
        sel = sel | hit
        score = jnp.where(hit, -jnp.inf, score)
    sel_ref[:nsel, :] = jnp.where(sel, 0.0, NEG_INF).astype(sel_ref.dtype)
    if nsel < sel_ref.shape[0]:
        sel_ref[nsel:, :] = jnp.full((sel_ref.shape[0] - nsel, tq), NEG_INF, sel_ref.dtype)


def _imp_weights(S):
    nsel, ncmp_pad = S // SEL_LEN, S // CMP_STRIDE
    ncmp = ncmp_pad - (CMP_LEN // CMP_STRIDE) + 1
    w = np.zeros((nsel, ncmp_pad), np.float32)
    for jj in range(nsel):
        for off in range(-(CMP_LEN // CMP_STRIDE - 1), SEL_RATIO):
            i = SEL_RATIO * jj + off
            lo = max(off * CMP_STRIDE, 0)
            hi = min(off * CMP_STRIDE + CMP_LEN - 1, SEL_LEN - 1)
            if 0 <= i < ncmp:
                w[jj, i] = (hi - lo + 1) / CMP_STRIDE
    return jnp.asarray(w, MXU_DTYPE)


def _nsa_cmp(qn, kcmp, vcmpT, gates5, B, S):
    tq = NSA_TQ
    nq = S // tq
    ncmp = S // CMP_STRIDE
    nsel = S // SEL_LEN
    wimp = _imp_weights(S)
    T = B * S
    qdim = NSA_REP * NSA_HEAD_DIM
    nvar = max([v for v in (4, 2) if nq % v == 0 and ncmp % (128 * v) == 0 and nsel % (16 * v) == 0
                and nsel // v >= SEL_TOPK] + [1])
    return pl.pallas_call(
        functools.partial(_nsa_cmp_kernel, tq=tq, sel_k=min(SEL_TOPK, nsel), nq=nq, nvar=nvar),
        grid=(B, NSA_GROUPS, nq),
        in_specs=[
            pl.BlockSpec((None, qdim, tq), lambda b, g, i: (b * nq + i, g, 0)),
            pl.BlockSpec((None, None, ncmp, NSA_HEAD_DIM), lambda b, g, i: (b, g, 0, 0)),
            pl.BlockSpec((None, NSA_HEAD_DIM, ncmp), lambda b, g, i: (b, g, 0)),
            pl.BlockSpec((nsel, ncmp), lambda b, g, i: (0, 0)),
            pl.BlockSpec((None, 3, None, NSA_REP, tq), lambda b, g, i: (b * nq + i, 0, g, 0, 0)),
        ],
        out_specs=(
            pl.BlockSpec((None, qdim, tq), lambda b, g, i: (b * nq + i, g, 0)),
            pl.BlockSpec((None, None, None, nsel, tq), lambda b, g, i: (b, g, i, 0, 0)),
        ),
        out_shape=(
            jax.ShapeDtypeStruct((T // tq, NSA_Q_DIM, tq), F32),
            jax.ShapeDtypeStruct((B, NSA_GROUPS, nq, nsel, tq), MXU_DTYPE),
        ),
        compiler_params=_params("parallel", "parallel", "parallel"), name="nsa_cmp_select",
    )(qn, kcmp, vcmpT, wimp, gates5)


def _flash_step(carry, s, vT):
    m, acc = carry
    m_new = jnp.maximum(m, jnp.max(s, axis=0, keepdims=True))
    alpha = jnp.exp2(m - m_new)
    p = jnp.exp2(s - m_new).astype(MXU_DTYPE)
    return m_new, alpha * acc + _dot(vT, p)


def _flash_init(dv, nq):
    return (jnp.full((1, nq), NEG_INF, F32), jnp.zeros((dv + V_PAD, nq), F32))


def _flash_finish(carry, dv):
    _, acc = carry
    return acc[:dv] / acc[dv:dv + 1]


def _append_ones(vT):
    n = vT.shape[1]
    ones = (lax.broadcasted_iota(jnp.int32, (V_PAD, n), 0) == 0).astype(vT.dtype)
    return jnp.concatenate([vT, ones], axis=0)


def _flash_causal(qi, unroll, tk, scores, values, diff, s_scr, carry):
    for u in range(unroll):
        s_scr[u] = scores(u)

    def body(i, carry):
        for u in range(unroll):
            m, acc = carry
            s = s_scr[u]
            m_new = jnp.maximum(m, jnp.max(s, axis=0, keepdims=True))
            alpha = jnp.exp2(m - m_new)
            p = jnp.exp2(s - m_new).astype(MXU_DTYPE)
            s_scr[u] = scores((i + 1) * unroll + u)
            carry = (m_new, alpha * acc + _dot(values(i * unroll + u), p))
        return carry

    carry = lax.fori_loop(0, qi, body, carry)
    for d in range(unroll):
        s = jnp.where(diff <= -d * tk, s_scr[d], NEG_INF)
        carry = _flash_step(carry, s, values(qi * unroll + d))
    return carry


def _nsa_attn_kernel(qr_ref, ks_ref, vs_ref, kw_ref, vw_ref, sel_ref, oc_ref, gt_ref, o_ref, s_scr, *, tq, tk):
    qi = pl.program_id(2)
    lanes = NSA_REP * tq
    dh = NSA_HEAD_DIM
    unroll = tq // tk
    qT = _stack_heads(qr_ref)
    q_aug = jnp.concatenate([qT, jnp.concatenate([sel_ref[...]] * NSA_REP, axis=1)], axis=0)
    tl = lax.broadcasted_iota(jnp.int32, (1, lanes), 1) % tq
    diff = lax.broadcasted_iota(jnp.int32, (tk, 1), 0) - tl

    def win_scores(d):
        kt = qi * unroll + d
        mask = diff <= -d * tk if d >= 0 else None
        lo_thr = -d * tk - WINDOW
        if d < 0:
            lo = diff > jnp.where(kt >= 0, lo_thr, 2 ** 30)
            kt = jnp.maximum(kt, 0)
        else:
            lo = diff > lo_thr if lo_thr >= -(tq - 1) else None
        if lo is not None:
            mask = lo if mask is None else mask & lo
        s = _dot(kw_ref[pl.ds(pl.multiple_of(kt * tk, tk), tk), :], qT)
        return s if mask is None else jnp.where(mask, s, NEG_INF)

    order = [0] + list(range(1, unroll)) + list(range(-(WINDOW // tk), 0))
    carry = _flash_init(dh, lanes)
    nxt = win_scores(order[0])
    for i, d in enumerate(order):
        m, acc = carry
        s = nxt
        m_new = jnp.maximum(m, jnp.max(s, axis=0, keepdims=True))
        alpha = jnp.exp2(m - m_new)
        p = jnp.exp2(s - m_new).astype(MXU_DTYPE)
        if i + 1 < len(order):
            nxt = win_scores(order[i + 1])
        carry = (m_new, alpha * acc + _dot(vw_ref[jnp.maximum(qi * unroll + d, 0)], p))
    o_w = _flash_finish(carry, dh)

    def sel_scores(kt):
        return _dot(ks_ref[pl.ds(pl.multiple_of(kt * tk, tk), tk), :], q_aug)

    o_s = _flash_finish(_flash_causal(qi, unroll, tk, sel_scores, lambda kt: vs_ref[kt], diff, s_scr,
                                      _flash_init(dh, lanes)), dh)
    for r in range(NSA_REP):
        ls = slice(r * tq, (r + 1) * tq)
        rs = slice(r * dh, (r + 1) * dh)
        o = oc_ref[rs, :] + gt_ref[1, r:r + 1, :] * o_s[:, ls] + gt_ref[2, r:r + 1, :] * o_w[:, ls]
        o_ref[rs, :] = o.astype(o_ref.dtype)


def _nsa_attn(qr, ks, vsT, kw, vwT, selb, ocg, gates5, B, S):
    tq, tk = NSA_TQ, NSA_TK
    nq = S // tq
    nkt = S // tk
    nsel = S // SEL_LEN
    T = B * S
    qdim = NSA_REP * NSA_HEAD_DIM
    assert tq % tk == 0 and WINDOW % tk == 0 and tq <= WINDOW
    kspec = pl.BlockSpec((None, None, S, NSA_HEAD_DIM), lambda b, g, i: (b, g, 0, 0))
    kaspec = pl.BlockSpec((None, None, S, NSA_HEAD_DIM + nsel), lambda b, g, i: (b, g, 0, 0))
    vspec = pl.BlockSpec((nkt, NSA_HEAD_DIM + V_PAD, tk), lambda b, g, i: (b, g, 0))
    qspec = pl.BlockSpec((None, qdim, tq), lambda b, g, i: (b * nq + i, g, 0))
    return pl.pallas_call(
        functools.partial(_nsa_attn_kernel, tq=tq, tk=tk),
        grid=(B, NSA_GROUPS, nq),
        in_specs=[
            qspec, kaspec, vspec, kspec, vspec,
            pl.BlockSpec((None, None, None, nsel, tq), lambda b, g, i: (b, g, i, 0, 0)),
            qspec,
            pl.BlockSpec((None, 3, None, NSA_REP, tq), lambda b, g, i: (b * nq + i, 0, g, 0, 0)),
        ],
        out_specs=qspec,
        out_shape=jax.ShapeDtypeStruct((T // tq, NSA_Q_DIM, tq), MXU_DTYPE),
        scratch_shapes=[pltpu.VMEM((tq // tk, tk, NSA_REP * tq), F32)],
        compiler_params=_params("parallel", "parallel", "arbitrary"), name="nsa_attn",
    )(qr, ks, vsT, kw, vwT, selb, ocg, gates5)


def _out_proj_kernel(a_ref, x_ref, w_ref, o_ref):
    nsub, _, tq = a_ref.shape
    for s in range(nsub):
        yT = _dot(w_ref[...], a_ref[s])
        o_ref[s * tq:(s + 1) * tq, :] = x_ref[s * tq:(s + 1) * tq, :] + yT.T


def _out_proj(aT, x2, w_out):
    T, D = x2.shape
    nt, feat, tq = aT.shape
    tm = ROW_TILE
    wT = w_out.T.astype(MXU_DTYPE)
    return pl.pallas_call(
        _out_proj_kernel,
        grid=(T // tm,),
        in_specs=[pl.BlockSpec((tm // tq, feat, tq), lambda i: (i, 0, 0)),
                  pl.BlockSpec((tm, D), lambda i: (i, 0)),
                  pl.BlockSpec((D, feat), lambda i: (0, 0))],
        out_specs=pl.BlockSpec((tm, D), lambda i: (i, 0)),
        out_shape=jax.ShapeDtypeStruct((T, D), F32),
        compiler_params=_params("parallel"), name="out_proj",
    )(aT, x2, wT)


def _mla_proj_kernel(x_ref, g_ref, win_ref, wuq_ref, wukv_ref, cos_ref, sin_ref,
                     gq_ref, gkv_ref, gqn_ref, gqr_ref, gkn_ref, gkr_ref,
                     q_ref, k_ref, v_ref, *, tq, tk):
    tm = x_ref.shape[0]
    h = _rms_lanes(x_ref[...], g_ref[...]).astype(MXU_DTYPE)
    cos = cos_ref[...]
    sin = sin_ref[...]
    cT = _dot_nt(win_ref[...], h)
    cq = _rms_rows(cT[:MLA_Q_LORA], gq_ref[...]).astype(MXU_DTYPE)
    ckv = _rms_rows(cT[MLA_Q_LORA:MLA_Q_LORA + MLA_KV_LORA], gkv_ref[...]).astype(MXU_DTYPE)
    kr = _rope_rows(_rms_rows(cT[MLA_Q_LORA + MLA_KV_LORA:], gkr_ref[...]), cos, sin)

    scale = MLA_QK ** -0.5 * LOG2E
    q3 = _dot(wuq_ref[...], cq).reshape(MLA_HEADS, MLA_QK, tm)
    qn = _rms_rows(q3[:, :MLA_NOPE, :], gqn_ref[...])
    qr = _rope_rows(_rms_rows(q3[:, MLA_NOPE:, :], gqr_ref[...]), cos, sin)
    q = (jnp.concatenate([qn, qr], axis=1) * scale).astype(MXU_DTYPE).reshape(MLA_HEADS * MLA_QK, tm)
    for s in range(tm // tq):
        q_ref[s] = q[:, s * tq:(s + 1) * tq]

    kv3 = _dot(wukv_ref[...], ckv).reshape(MLA_HEADS, MLA_NOPE + MLA_V, tm)
    v3 = kv3[:, MLA_NOPE:, :].astype(MXU_DTYPE)
    vT = jnp.concatenate([_append_ones(v3[hh]) for hh in range(MLA_HEADS)], axis=0)
    for s in range(tm // tk):
        v_ref[s] = vT[:, s * tk:(s + 1) * tk]
    kn = _rms_rows(kv3[:, :MLA_NOPE, :], gkn_ref[...])
    kr_t = kr.T.astype(MXU_DTYPE)
    for hh in range(MLA_HEADS):
        k_ref[hh] = jnp.concatenate([kn[hh].T.astype(MXU_DTYPE), kr_t], axis=1)


def _mla_proj(x2, g, w_in, q_lat_norm, kv_lat_norm, w_uq, w_ukv, qn_norm, qr_norm, kn_norm, kr_norm,
              cosT, sinT, B, S):
    T, D = x2.shape
    tm, tq, tk = ROW_TILE, ROW_TILE, MLA_TK
    nps = S // tm
    winT = w_in.T.astype(MXU_DTYPE)
    wuqT = w_uq.T.astype(MXU_DTYPE)
    wukvT = w_ukv.T.astype(MXU_DTYPE)
    full = lambda shape: pl.BlockSpec(shape, lambda i: (0,) * len(shape))
    col = lambda v: v.reshape(-1, 1)
    return pl.pallas_call(
        functools.partial(_mla_proj_kernel, tq=tq, tk=tk),
        grid=(T // tm,),
        in_specs=[pl.BlockSpec((tm, D), lambda i: (i, 0)), full((1, D)),
                  full(winT.shape), full(wuqT.shape), full(wukvT.shape),
                  pl.BlockSpec((MLA_ROPE // 2, tm), lambda i: (0, i % nps)),
                  pl.BlockSpec((MLA_ROPE // 2, tm), lambda i: (0, i % nps)),
                  full((MLA_Q_LORA, 1)), full((MLA_KV_LORA, 1)), full((MLA_NOPE, 1)),
                  full((MLA_ROPE, 1)), full((MLA_NOPE, 1)), full((MLA_ROPE, 1))],
        out_specs=(pl.BlockSpec((tm // tq, MLA_HEADS * MLA_QK, tq), lambda i: (i, 0, 0)),
                   pl.BlockSpec((None, MLA_HEADS, tm, MLA_QK), lambda i: (i // nps, 0, i % nps, 0)),
                   pl.BlockSpec((tm // tk, MLA_HEADS * (MLA_V + V_PAD), tk), lambda i: (i, 0, 0))),
        out_shape=(jax.ShapeDtypeStruct((T // tq, MLA_HEADS * MLA_QK, tq), MXU_DTYPE),
                   jax.ShapeDtypeStruct((B, MLA_HEADS, S, MLA_QK), MXU_DTYPE),
                   jax.ShapeDtypeStruct((T // tk, MLA_HEADS * (MLA_V + V_PAD), tk), MXU_DTYPE)),
        compiler_params=_params("parallel"), name="mla_proj",
    )(x2, g.reshape(1, D), winT, wuqT, wukvT, cosT, sinT,
      col(q_lat_norm), col(kv_lat_norm), col(qn_norm), col(qr_norm), col(kn_norm), col(kr_norm))


def _mla_attn_kernel(q_ref, k_ref, v_ref, o_ref, s_scr, *, tq, tk):
    qi = pl.program_id(2)
    unroll = tq // tk
    nsub, _, tl = q_ref.shape
    qT = jnp.concatenate([q_ref[s] for s in range(nsub)], axis=1)
    diff = lax.broadcasted_iota(jnp.int32, (tk, 1), 0) - lax.broadcasted_iota(jnp.int32, (1, tq), 1)

    def scores(kt):
        return _dot(k_ref[pl.ds(pl.multiple_of(kt * tk, tk), tk), :], qT)

    o = _flash_finish(_flash_causal(qi, unroll, tk, scores, lambda kt: v_ref[kt], diff, s_scr,
                                    _flash_init(MLA_V, tq)), MLA_V).astype(o_ref.dtype)
    for s in range(nsub):
        o_ref[s] = o[:, s * tl:(s + 1) * tl]


def _mla_attn(qT, k, vT, B, S):
    tq, tk = MLA_TQ, MLA_TK
    tl = qT.shape[2]
    nsub = tq // tl
    nq = S // tq
    nkt = S // tk
    T = B * S
    return pl.pallas_call(
        functools.partial(_mla_attn_kernel, tq=tq, tk=tk),
        grid=(B, MLA_HEADS, nq),
        in_specs=[pl.BlockSpec((nsub, MLA_QK, tl), lambda b, h, i: (b * nq + i, h, 0)),
                  pl.BlockSpec((None, None, S, MLA_QK), lambda b, h, i: (b, h, 0, 0)),
                  pl.BlockSpec((nkt, MLA_V + V_PAD, tk), lambda b, h, i: (b, h, 0))],
        out_specs=pl.BlockSpec((nsub, MLA_V, tl), lambda b, h, i: (b * nq + i, h, 0)),
        out_shape=jax.ShapeDtypeStruct((T // tl, MLA_HEADS * MLA_V, tl), MXU_DTYPE),
        scratch_shapes=[pltpu.VMEM((tq // tk, tk, tq), F32)],
        compiler_params=_params("parallel", "parallel", "arbitrary"), name="mla_attn",
    )(qT, k, vT)


def _moe_router_kernel(x_ref, g_ref, w_ref, b_ref, id_ref, wt_ref, cnt_ref):
    tm = x_ref.shape[0]
    h = _rms_lanes(x_ref[...], g_ref[...]).astype(MXU_DTYPE)
    lg = _dot_nt(w_ref[...], h) + b_ref[...]
    gl = lg[:MOE_GROUPS]
    ge = jnp.exp(gl - jnp.max(gl, axis=0, keepdims=True))
    gp = ge / jnp.sum(ge, axis=0, keepdims=True)
    rid = lax.broadcasted_iota(jnp.int32, (MOE_GROUPS, 1), 0)
    g_w = jnp.max(gp, axis=0, keepdims=True)
    grp = jnp.min(jnp.where(gp == g_w, rid, MOE_GROUPS), axis=0, keepdims=True)
    el = jnp.zeros((MOE_EPG, tm), F32)
    for gi in range(MOE_GROUPS):
        rows = lg[MOE_GROUPS + gi * MOE_EPG:MOE_GROUPS + (gi + 1) * MOE_EPG]
        el = jnp.where(grp == gi, rows, el)
    ee = jnp.exp(el - jnp.max(el, axis=0, keepdims=True))
    ep = ee / jnp.sum(ee, axis=0, keepdims=True)
    p1 = jnp.max(ep, axis=0, keepdims=True)
    i1 = jnp.min(jnp.where(ep == p1, rid, MOE_EPG), axis=0, keepdims=True)
    ep2 = jnp.where(rid == i1, -1.0, ep)
    p2 = jnp.max(ep2, axis=0, keepdims=True)
    i2 = jnp.min(jnp.where(ep2 == p2, rid, MOE_EPG), axis=0, keepdims=True)
    den = p1 + p2
    zi = jnp.zeros((1, tm), jnp.int32)
    zf = jnp.zeros((1, tm), F32)
    e1 = grp * MOE_EPG + i1
    e2 = grp * MOE_EPG + i2
    id_ref[...] = jnp.concatenate([e1, e2] + [zi] * 6, axis=0)
    wt_ref[...] = jnp.concatenate([g_w * p1 / den, g_w * p2 / den] + [zf] * 6, axis=0)
    eid = lax.broadcasted_iota(jnp.int32, (MOE_EXPERTS, 1), 0)
    cnt_ref[...] = jnp.sum(((eid == e1) | (eid == e2)).astype(F32), axis=1, keepdims=True)


def _moe_router(x2, g, w_group, b_group, w_expert, b_expert):
    T, D = x2.shape
    tm = ROW_TILE
    nrow = 128
    pad = nrow - MOE_GROUPS - MOE_EXPERTS
    wT = jnp.concatenate([w_group, w_expert, jnp.zeros((D, pad), w_group.dtype)], axis=1).T.astype(MXU_DTYPE)
    bias = jnp.concatenate([b_group, b_expert, jnp.zeros((pad,), b_group.dtype)]).reshape(nrow, 1)
    return pl.pallas_call(
        _moe_router_kernel,
        grid=(T // tm,),
        in_specs=[pl.BlockSpec((tm, D), lambda i: (i, 0)), pl.BlockSpec((1, D), lambda i: (0, 0)),
                  pl.BlockSpec((nrow, D), lambda i: (0, 0)), pl.BlockSpec((nrow, 1), lambda i: (0, 0))],
        out_specs=(pl.BlockSpec((None, 8, tm), lambda i: (i, 0, 0)), pl.BlockSpec((None, 8, tm), lambda i: (i, 0, 0)),
                   pl.BlockSpec((None, MOE_EXPERTS, 1), lambda i: (i, 0, 0))),
        out_shape=(jax.ShapeDtypeStruct((T // tm, 8, tm), jnp.int32), jax.ShapeDtypeStruct((T // tm, 8, tm), F32),
                   jax.ShapeDtypeStruct((T // tm, MOE_EXPERTS, 1), F32)),
        compiler_params=_params("parallel"), name="moe_router",
    )(x2, g.reshape(1, D), wT, bias)


def _moe_plan_kernel(id_ref, base_ref, tri_ref, d_ref):
    tm = id_ref.shape[1]
    eid = lax.broadcasted_iota(jnp.int32, (MOE_EXPERTS, 1), 0)
    f0 = (eid == id_ref[0:1, :]).astype(F32)
    f1 = (eid == id_ref[1:2, :]).astype(F32)
    pre0 = _dot(f0.astype(MXU_DTYPE), tri_ref[...])
    pre1 = _dot(f1.astype(MXU_DTYPE), tri_ref[...])
    base = base_ref[...]
    tot0 = jnp.sum(f0, axis=1, keepdims=True)
    d0 = jnp.sum(f0 * (base + pre0), axis=0, keepdims=True)
    d1 = jnp.sum(f1 * (base + tot0 + pre1), axis=0, keepdims=True)
    zf = jnp.zeros((1, tm), F32)
    d_ref[...] = jnp.concatenate([d0, d1] + [zf] * 6, axis=0).astype(jnp.int32)


def _moe_plan(ids, cnt):
    nt, _, tm = ids.shape
    blk = MOE_BLOCK
    c = cnt.reshape(nt, MOE_EXPERTS)
    tot = jnp.sum(c, axis=0)
    padded = jnp.ceil(tot / blk) * blk
    pad_ends = jnp.cumsum(padded)
    tile_base = (pad_ends - padded)[None, :] + jnp.cumsum(c, axis=0) - c
    n_blocks = nt * tm * MOE_TOP_K // blk + MOE_EXPERTS
    first_row = jnp.arange(n_blocks, dtype=F32) * blk
    block_e = jnp.minimum(jnp.sum((pad_ends[None, :] <= first_row[:, None]).astype(jnp.int32), axis=1),
                          MOE_EXPERTS - 1)
    n_used = (pad_ends[-1] / blk).astype(jnp.int32).reshape(1)
    tri = jnp.asarray(np.arange(tm)[:, None] < np.arange(tm)[None, :], MXU_DTYPE)
    dest = pl.pallas_call(
        _moe_plan_kernel,
        grid=(nt,),
        in_specs=[pl.BlockSpec((None, 8, tm), lambda i: (i, 0, 0)),
                  pl.BlockSpec((None, MOE_EXPERTS, 1), lambda i: (i, 0, 0)),
                  pl.BlockSpec((tm, tm), lambda i: (0, 0))],
        out_specs=pl.BlockSpec((None, 8, tm), lambda i: (i, 0, 0)),
        out_shape=jax.ShapeDtypeStruct((nt, 8, tm), jnp.int32),
        compiler_params=_params("parallel"), name="moe_plan",
    )(ids, tile_base.reshape(nt, MOE_EXPERTS, 1), tri)
    return dest, block_e, n_used, n_blocks


def _moe_scatter_kernel(d_ref, x_ref, xs_prev, xs_hbm, sem):
    del xs_prev
    n = x_ref.shape[0]

    def start(j, c):
        src = x_ref.at[pl.ds(j, 1), :]
        for k in range(MOE_TOP_K):
            pltpu.make_async_copy(src, xs_hbm.at[pl.ds(d_ref[0, 0, k * n + j], 1), :], sem.at[0]).start()
        return c

    lax.fori_loop(0, n, start, 0, unroll=8)
    for k in range(MOE_TOP_K):
        pltpu.make_async_copy(x_ref, xs_hbm.at[pl.ds(0, n), :], sem.at[0]).wait()


def _moe_scatter(x2, dest, xs_buf):
    T, D = x2.shape
    n_slots = xs_buf.shape[0]
    nt, _, tm = dest.shape
    d3 = dest[:, :MOE_TOP_K, :].reshape(nt, 1, MOE_TOP_K * tm)
    return pl.pallas_call(
        _moe_scatter_kernel,
        grid=(nt,),
        in_specs=[pl.BlockSpec((1, 1, MOE_TOP_K * tm), lambda i: (i, 0, 0), memory_space=pltpu.SMEM),
                  pl.BlockSpec((tm, D), lambda i: (i, 0)), pl.BlockSpec(memory_space=pl.ANY)],
        out_specs=pl.BlockSpec(memory_space=pl.ANY),
        out_shape=jax.ShapeDtypeStruct((n_slots, D), F32),
        scratch_shapes=[pltpu.SemaphoreType.DMA((1,))],
        input_output_aliases={2: 0},
        compiler_params=_params("arbitrary"), name="moe_scatter",
    )(d3, x2, xs_buf)


def _moe_expert_kernel(be_ref, nu_ref, x_ref, g_ref, wg_ref, wu_ref, wd_ref, y_ref, wg_s, wu_s, wd_s):
    i = pl.program_id(0)

    @pl.when((i == 0) | (be_ref[i] != be_ref[jnp.maximum(i - 1, 0)]))
    def _():
        wg_s[...] = wg_ref[...].astype(MXU_DTYPE)
        wu_s[...] = wu_ref[...].astype(MXU_DTYPE)
        wd_s[...] = wd_ref[...].astype(MXU_DTYPE)

    @pl.when(i < nu_ref[0])
    def _():
        h = _rms_lanes(x_ref[...], g_ref[...]).astype(MXU_DTYPE)
        a = _dot(h, wg_s[...])
        u = _dot(h, wu_s[...])
        mid = (jax.nn.silu(a) * u).astype(MXU_DTYPE)
        y_ref[...] = _dot(mid, wd_s[...])

    @pl.when(i >= nu_ref[0])
    def _():
        y_ref[...] = jnp.zeros_like(y_ref)


def _moe_experts(xs, g, block_e, n_used, w_gate, w_up, w_down, layer):
    n_slots, D = xs.shape
    blk = MOE_BLOCK
    n_blocks = n_slots // blk
    grid_spec = pltpu.PrefetchScalarGridSpec(
        num_scalar_prefetch=2,
        grid=(n_blocks,),
        in_specs=[
            pl.BlockSpec((blk, D), lambda i, be, nu: (i, 0)),
            pl.BlockSpec((1, D), lambda i, be, nu: (0, 0)),
            pl.BlockSpec((None, None, D, MOE_FF), lambda i, be, nu: (layer, be[i], 0, 0)),
            pl.BlockSpec((None, None, D, MOE_FF), lambda i, be, nu: (layer, be[i], 0, 0)),
            pl.BlockSpec((None, None, MOE_FF, D), lambda i, be, nu: (layer, be[i], 0, 0)),
        ],
        out_specs=pl.BlockSpec((blk, D), lambda i, be, nu: (i, 0)),
        scratch_shapes=[pltpu.VMEM((D, MOE_FF), MXU_DTYPE), pltpu.VMEM((D, MOE_FF), MXU_DTYPE),
                        pltpu.VMEM((MOE_FF, D), MXU_DTYPE)],
    )
    return pl.pallas_call(
        _moe_expert_kernel, grid_spec=grid_spec,
        out_shape=jax.ShapeDtypeStruct((n_slots, D), F32),
        compiler_params=_params("arbitrary"), name="moe_experts",
    )(block_e, n_used, xs, g.reshape(1, D), w_gate, w_up, w_down)


def _moe_combine_kernel(d_ref, y_hbm, x_ref, w_ref, o_ref, ybuf, sem):
    tc = x_ref.shape[0]

    def start(r, c):
        pltpu.make_async_copy(y_hbm.at[pl.ds(d_ref[0, 0, r], 1), :], ybuf.at[pl.ds(r, 1), :], sem.at[0]).start()
        return c

    lax.fori_loop(0, MOE_TOP_K * tc, start, 0, unroll=8)
    pltpu.make_async_copy(y_hbm.at[pl.ds(0, MOE_TOP_K * tc), :], ybuf, sem.at[0]).wait()
    w = w_ref[...]
    o_ref[...] = x_ref[...] + w[:, 0:1] * ybuf[:tc, :] + w[:, 1:2] * ybuf[tc:, :]


def _moe_combine(x2, y, dest, wts):
    T, D = x2.shape
    nt, _, tm = dest.shape
    tc = MOE_TC
    d3 = dest[:, :MOE_TOP_K, :].reshape(nt, MOE_TOP_K, tm // tc, tc).transpose(0, 2, 1, 3).reshape(T // tc, 1, MOE_TOP_K * tc)
    w2 = wts[:, :MOE_TOP_K, :].transpose(0, 2, 1).reshape(T, MOE_TOP_K)
    return pl.pallas_call(
        _moe_combine_kernel,
        grid=(T // tc,),
        in_specs=[pl.BlockSpec((1, 1, MOE_TOP_K * tc), lambda i: (i, 0, 0), memory_space=pltpu.SMEM),
                  pl.BlockSpec(memory_space=pl.ANY),
                  pl.BlockSpec((tc, D), lambda i: (i, 0)),
                  pl.BlockSpec((tc, MOE_TOP_K), lambda i: (i, 0))],
        out_specs=pl.BlockSpec((tc, D), lambda i: (i, 0)),
        out_shape=jax.ShapeDtypeStruct((T, D), F32),
        scratch_shapes=[pltpu.VMEM((MOE_TOP_K * tc, D), F32), pltpu.SemaphoreType.DMA((1,))],
        compiler_params=_params("arbitrary"), name="moe_combine",
    )(d3, y, x2, w2)


def _hier_moe(x2, g, w_group, b_group, w_expert, b_expert, w_gate, w_up, w_down, layer, xs_buf):
    ids, wts, cnt = _moe_router(x2, g, w_group, b_group, w_expert, b_expert)
    dest, block_e, n_used, n_blocks = _moe_plan(ids, cnt)
    assert xs_buf.shape[0] == n_blocks * MOE_BLOCK
    xs = _moe_scatter(x2, dest, xs_buf)
    y = _moe_experts(xs, g, block_e, n_used, w_gate, w_up, w_down, layer)
    return _moe_combine(x2, y, dest, wts), xs


def _rope_tables_T(seq, dim):
    inv = 1.0 / (ROPE_THETA ** (jnp.arange(0, dim, 2, dtype=F32) / dim))
    ang = jnp.arange(seq, dtype=F32)[:, None] * inv[None, :]
    return jnp.cos(ang).T, jnp.sin(ang).T


def _nsa_layer(x2, g, w_in, cmp_pos, cmp_w1, cmp_w2, q_norm, k_norm, w_out, cosT, sinT, B, S):
    qn, qr, ks, kw, vsT, vwT, gT, kc, vc = _nsa_proj(x2, g, w_in, q_norm, k_norm, cosT, sinT, B, S)
    kcmp = _compress(kc, cmp_pos[0], cmp_w1[0], cmp_w2[0], k_norm[0], B, S, True)
    vcmpT = _compress(vc, cmp_pos[1], cmp_w1[1], cmp_w2[1], k_norm[0], B, S, False)
    T = B * S
    gates5 = gT.reshape(T // NSA_TQ, 3, NSA_GROUPS, NSA_REP, NSA_TQ)
    ocg, selb = _nsa_cmp(qn, kcmp, vcmpT, gates5, B, S)
    aT = _nsa_attn(qr, ks, vsT, kw, vwT, selb, ocg, gates5, B, S)
    return _out_proj(aT, x2, w_out)


def _mla_layer(x2, g, w_in, q_lat_norm, kv_lat_norm, w_uq, w_ukv, qn_norm, qr_norm, kn_norm, kr_norm,
               w_out, cosT, sinT, B, S):
    qT, k, vT = _mla_proj(x2, g, w_in, q_lat_norm, kv_lat_norm, w_uq, w_ukv, qn_norm, qr_norm, kn_norm, kr_norm,
                          cosT, sinT, B, S)
    aT = _mla_attn(qT, k, vT, B, S)
    return _out_proj(aT, x2, w_out)


def kernel(x, norm_mix, norm_ffn, nsa_w_in, nsa_cmp_pos, nsa_cmp_w1, nsa_cmp_w2, nsa_q_norm, nsa_k_norm, nsa_w_out, mla_w_in, mla_q_lat_norm, mla_kv_lat_norm, mla_w_uq, mla_w_ukv, mla_q_nope_norm, mla_q_rope_norm, mla_k_nope_norm, mla_k_rope_norm, mla_w_out, moe_w_group, moe_b_group, moe_w_expert, moe_b_expert, moe_w_gate, moe_w_up, moe_w_down):
    B, S, D = x.shape
    depth = norm_mix.shape[0]
    assert S % max(ROW_TILE, NSA_TQ, MLA_TQ) == 0 and MLA_TQ % ROW_TILE == 0 and S >= WINDOW
    cosT, sinT = _rope_tables_T(S, NSA_HEAD_DIM)
    x2 = x.reshape(B * S, D)
    xs_buf = jnp.zeros((B * S * MOE_TOP_K + MOE_EXPERTS * MOE_BLOCK, D), F32)
    for i in range(depth):
        j = i // 2
        if i % 2 == 0:
            x2 = _nsa_layer(x2, norm_mix[i], nsa_w_in[j], nsa_cmp_pos[j], nsa_cmp_w1[j], nsa_cmp_w2[j],
                            nsa_q_norm[j], nsa_k_norm[j], nsa_w_out[j], cosT, sinT, B, S)
        else:
            x2 = _mla_layer(x2, norm_mix[i], mla_w_in[j], mla_q_lat_norm[j], mla_kv_lat_norm[j], mla_w_uq[j],
                            mla_w_ukv[j], mla_q_nope_norm[j], mla_q_rope_norm[j], mla_k_nope_norm[j],
                            mla_k_rope_norm[j], mla_w_out[j], cosT, sinT, B, S)
        x2, xs_buf = _hier_moe(x2, norm_ffn[i], moe_w_group[i], moe_b_group[i], moe_w_expert[i], moe_b_expert[i],
                               moe_w_gate, moe_w_up, moe_w_down, i, xs_buf)
    return x2.reshape(B, S, D)
```
